```python
import jax, jax.numpy as jnp
from jax import lax
import numpy as np

D_MODEL = 1024
BATCH = 16
SEQ = 2048
DEPTH = 2

CHUNK = 64
N_HEADS = 16
HEAD_DIM = D_MODEL // N_HEADS
Q_BLOCK = 128
N_MIXERS = 2
N_FOX = DEPTH // 2
N_EXPERTS = 32
TOP_K = 4
D_FF = D_MODEL
SWIGLU_LIMIT = 7.0
SWIGLU_ALPHA = 1.702
PLE_DIM = 256
EXPERT_BLOCK = 256
RMS_EPS = 1e-6

kernel_name = "hybrid_stickbreak_fox_moe_ple"


def rmsnorm(x, g):
    xf = x.astype(jnp.float32)
    var = jnp.mean(xf * xf, axis=-1, keepdims=True)
    return (xf * lax.rsqrt(var + RMS_EPS) * g.astype(jnp.float32)).astype(x.dtype)


def split_heads(t):
    b, s, _ = t.shape
    return t.reshape(b, s, N_HEADS, HEAD_DIM).transpose(0, 2, 1, 3)


def merge_heads(t):
    b, h, s, d = t.shape
    return t.transpose(0, 2, 1, 3).reshape(b, s, h * d)


def stick_breaking_attention(q, k, v):
    seq = q.shape[2]
    scale = HEAD_DIM ** -0.5
    outs = []
    for t0 in range(0, seq, Q_BLOCK):
        t1 = t0 + Q_BLOCK
        z = jnp.einsum('bhqd,bhkd->bhqk', q[:, :, t0:t1], k[:, :, :t1],
                       preferred_element_type=jnp.float32) * scale
        strict = jnp.arange(t1)[None, :] < jnp.arange(t0, t1)[:, None]
        log_not = jnp.where(strict, jax.nn.log_sigmoid(-z), 0.0)
        between = lax.cumsum(log_not, axis=3, reverse=True) - log_not
        a = jnp.where(strict, jnp.exp(jax.nn.log_sigmoid(z) + between), 0.0)
        outs.append(jnp.einsum('bhqk,bhkd->bhqd', a.astype(v.dtype), v[:, :, :t1]))
    return jnp.concatenate(outs, axis=2)


def forgetting_attention(q, k, v, log_f):
    seq = q.shape[2]
    scale = HEAD_DIM ** -0.5
    c = jnp.cumsum(log_f, axis=-1)
    outs = []
    for t0 in range(0, seq, Q_BLOCK):
        t1 = t0 + Q_BLOCK
        z = jnp.einsum('bhqd,bhkd->bhqk', q[:, :, t0:t1], k[:, :, :t1],
                       preferred_element_type=jnp.float32) * scale
        z = z + c[:, :, t0:t1, None] - c[:, :, None, :t1]
        causal = jnp.arange(t1)[None, :] <= jnp.arange(t0, t1)[:, None]
        a = jax.nn.softmax(jnp.where(causal, z, -jnp.inf), axis=-1)
        outs.append(jnp.einsum('bhqk,bhkd->bhqd', a.astype(v.dtype), v[:, :, :t1]))
    return jnp.concatenate(outs, axis=2)


def moe_ffn(xn, w_router, b_router, w_gu, b_gu, w_down, b_down):
    b, s, d = xn.shape
    n = b * s
    nk = n * TOP_K
    xt = xn.reshape(n, d)
    logits = jnp.matmul(xt, w_router, preferred_element_type=jnp.float32) + b_router.astype(jnp.float32)
    top_val, top_idx = lax.top_k(logits, TOP_K)
    gate = jax.nn.softmax(top_val, axis=-1).astype(xn.dtype)
    flat_e = top_idx.reshape(-1).astype(jnp.int32)
    flat_g = gate.reshape(-1)
    counts = jnp.bincount(flat_e, length=N_EXPERTS).astype(jnp.int32)
    padded = (counts + EXPERT_BLOCK - 1) // EXPERT_BLOCK * EXPERT_BLOCK
    pad_end = jnp.cumsum(padded)
    pad_start = pad_end - padded
    cnt_start = jnp.cumsum(counts) - counts
    order = jnp.argsort(flat_e).astype(jnp.int32)
    sorted_e = flat_e[order]
    rank = jnp.arange(nk, dtype=jnp.int32) - cnt_start[sorted_e]
    dest = pad_start[sorted_e] + rank
    n_blocks = -(-(nk + N_EXPERTS * (EXPERT_BLOCK - 1)) // EXPERT_BLOCK)
    n_rows = n_blocks * EXPERT_BLOCK
    row_token = jnp.full((n_rows,), n, jnp.int32).at[dest].set(order // TOP_K)
    row_gate = jnp.zeros((n_rows,), xn.dtype).at[dest].set(flat_g[order])
    block_start = jnp.arange(n_blocks, dtype=jnp.int32) * EXPERT_BLOCK
    block_expert = jnp.minimum(jnp.searchsorted(pad_end, block_start, side='right'), N_EXPERTS - 1)
    x_rows = jnp.concatenate([xt, jnp.zeros((1, d), xt.dtype)], axis=0)[row_token]
    x_rows = x_rows.reshape(n_blocks, EXPERT_BLOCK, d)

    def expert_block(args):
        xb, e = args
        hgu = xb @ w_gu[e] + b_gu[e]
        glu = jnp.minimum(hgu[:, :D_FF], SWIGLU_LIMIT)
        lin = jnp.clip(hgu[:, D_FF:], -SWIGLU_LIMIT, SWIGLU_LIMIT)
        act = glu * jax.nn.sigmoid(SWIGLU_ALPHA * glu) * (lin + 1.0)
        return act @ w_down[e] + b_down[e]

    y_rows = lax.map(expert_block, (x_rows, block_expert)).reshape(n_rows, d)
    y = jax.ops.segment_sum(y_rows * row_gate[:, None], row_token, num_segments=n + 1)[:n]
    return y.reshape(b, s, d)


def setup_inputs(seed: int = 0) -> dict:
    key = jax.random.key(seed)
    ks = jax.random.split(key, 20)
    D, E, F, H = D_MODEL, N_EXPERTS, D_FF, N_HEADS
    nrm = lambda k, shape, fan_in: jax.random.normal(k, shape, jnp.float32) * (fan_in ** -0.5)
    gain = lambda k, shape: 1.0 + 0.05 * jax.random.normal(k, shape, jnp.float32)
    return {
        "x": jax.random.normal(ks[0], (BATCH, SEQ, D), jnp.float32),
        "p": jax.random.normal(ks[1], (DEPTH, BATCH, SEQ, PLE_DIM), jnp.float32),
        "g_mix": gain(ks[2], (DEPTH, D)),
        "w_qkv": nrm(ks[3], (DEPTH, D, 3 * D), D),
        "w_fgate": nrm(ks[4], (N_FOX, D, H), D),
        "b_fgate": jax.random.uniform(ks[5], (N_FOX, H), jnp.float32, 1.0, 6.0),
        "w_o": nrm(ks[6], (DEPTH, D, D), D),
        "g_moe": gain(ks[7], (DEPTH, D)),
        "w_router": nrm(ks[8], (DEPTH, D, E), D),
        "b_router": 0.01 * jax.random.normal(ks[9], (DEPTH, E), jnp.float32),
        "w_gu": nrm(ks[10], (DEPTH, E, D, 2 * F), D),
        "b_gu": 0.01 * jax.random.normal(ks[11], (DEPTH, E, 2 * F), jnp.float32),
        "w_down": nrm(ks[12], (DEPTH, E, F, D), F),
        "b_down": 0.01 * jax.random.normal(ks[13], (DEPTH, E, D), jnp.float32),
        "g_ple": gain(ks[14], (DEPTH, D)),
        "w_ple_gate": nrm(ks[15], (DEPTH, D, D), D),
        "w_ple_proj": nrm(ks[16], (DEPTH, PLE_DIM, D), PLE_DIM),
        "g_final": gain(ks[17], (D,)),
    }


def reference(x, p, g_mix, w_qkv, w_fgate, b_fgate, w_o, g_moe, w_router, b_router,
              w_gu, b_gu, w_down, b_down, g_ple, w_ple_gate, w_ple_proj, g_final):
    D = D_MODEL
    h = x
    for i in range(DEPTH):
        hn = rmsnorm(h, g_mix[i])
        if i % N_MIXERS == 0:
            qkv = hn @ w_qkv[i]
            q, k, v = (split_heads(qkv[..., j * D:(j + 1) * D]) for j in range(3))
            o = stick_breaking_attention(q, k, v)
        else:
            j = i // N_MIXERS
            proj = hn @ jnp.concatenate([w_qkv[i], w_fgate[j]], axis=1)
            q, k, v = (split_heads(proj[..., m * D:(m + 1) * D]) for m in range(3))
            f_logit = (proj[..., 3 * D:] + b_fgate[j]).astype(jnp.float32)
            log_f = jax.nn.log_sigmoid(f_logit).transpose(0, 2, 1)
            o = forgetting_attention(q, k, v, log_f)
        h = h + merge_heads(o) @ w_o[i]
        h = h + moe_ffn(rmsnorm(h, g_moe[i]), w_router[i], b_router[i],
                        w_gu[i], b_gu[i], w_down[i], b_down[i])
        ple_gate = jax.nn.sigmoid(rmsnorm(h, g_ple[i]) @ w_ple_gate[i])
        h = h + ple_gate * (p[i] @ w_ple_proj[i])
    return rmsnorm(h, g_final)
```

```python
import functools

import jax
import jax.numpy as jnp
from jax import lax
from jax.experimental import pallas as pl
from jax.experimental.pallas import tpu as pltpu

N_HEADS = 16
HEAD_DIM = 64
N_EXPERTS = 32
TOP_K = 4
SWIGLU_LIMIT = 7.0
SWIGLU_ALPHA = 1.702
RMS_EPS = 1e-6

LANES = 128
HEADS_PER_STEP = LANES // HEAD_DIM
VMEM_LIMIT_BYTES = 48 * 1024 * 1024

ROW_BLOCK = 256
ATTN_BLOCK = 256
TOKEN_BLOCK = 512
GATHER_BLOCK = 256

ROUTE_IDX, ROUTE_GATE, ROUTE_RANK = 0, TOP_K, 2 * TOP_K

f32 = jnp.float32
bf16 = jnp.bfloat16


def _params(*semantics):
    return pltpu.CompilerParams(dimension_semantics=semantics, vmem_limit_bytes=VMEM_LIMIT_BYTES)


def _rmsnorm(x, g):
    var = jnp.mean(x * x, axis=-1, keepdims=True)
    return x * lax.rsqrt(var + RMS_EPS) * g


def _softplus(z):
    return jnp.maximum(z, 0.0) + jnp.log(1.0 + jnp.exp(-jnp.abs(z)))


def _split2(x):
    hi = x.astype(bf16)
    lo = (x - hi.astype(f32)).astype(bf16)
    return hi, lo


def _norm_proj_kernel(h_ref, g_ref, w_ref, *rest, with_gate):
    xn = _rmsnorm(h_ref[...], g_ref[...]).astype(bf16)
    if with_gate:
        wf_ref, o_ref, of_ref = rest
        of_ref[...] = jnp.dot(xn, wf_ref[...], preferred_element_type=f32)
    else:
        (o_ref,) = rest
    o_ref[...] = jnp.dot(xn, w_ref[...], preferred_element_type=f32).astype(o_ref.dtype)


def norm_proj(h, g, w, wf=None):
    n, d = h.shape
    m = w.shape[1]
    tm = min(TOKEN_BLOCK, n)
    with_gate = wf is not None
    in_specs = [pl.BlockSpec((tm, d), lambda i: (i, 0)),
                pl.BlockSpec((1, d), lambda i: (0, 0)),
                pl.BlockSpec((d, m), lambda i: (0, 0))]
    out_specs = [pl.BlockSpec((tm, m), lambda i: (i, 0))]
    out_shape = [jax.ShapeDtypeStruct((n, m), bf16)]
    args = [h, g.reshape(1, d), w]
    if with_gate:
        in_specs.append(pl.BlockSpec((d, LANES), lambda i: (0, 0)))
        out_specs.append(pl.BlockSpec((tm, LANES), lambda i: (i, 0)))
        out_shape.append(jax.ShapeDtypeStruct((n, LANES), f32))
        args.append(wf)
    out = pl.pallas_call(
        functools.partial(_norm_proj_kernel, with_gate=with_gate),
        grid=(n // tm,), in_specs=in_specs, out_specs=out_specs, out_shape=out_shape,
        compiler_params=_params("parallel"), name="norm_proj")(*args)
    return out if with_gate else out[0]


def _fox_cumsum_kernel(f_ref, b_ref, tri_ref, c_ref, carry_ref):
    @pl.when(pl.program_id(1) == 0)
    def _():
        carry_ref[...] = jnp.zeros_like(carry_ref)

    x = f_ref[0] + b_ref[...]
    log_f = jnp.minimum(x, 0.0) - jnp.log(1.0 + jnp.exp(-jnp.abs(x)))
    p1 = log_f.astype(bf16)
    r1 = log_f - p1.astype(f32)
    p2 = r1.astype(bf16)
    p3 = (r1 - p2.astype(f32)).astype(bf16)
    tri = tri_ref[...]
    c = (jnp.dot(tri, p1, preferred_element_type=f32)
         + jnp.dot(tri, p2, preferred_element_type=f32)
         + jnp.dot(tri, p3, preferred_element_type=f32)) + carry_ref[...]
    c_ref[0] = c
    carry_ref[...] = c[-1:, :]


def fox_cumsum(f_logit, b_pad, batch, seq):
    ts = min(ATTN_BLOCK, seq)
    tri = (jnp.arange(ts)[None, :] <= jnp.arange(ts)[:, None]).astype(bf16)
    return pl.pallas_call(
        _fox_cumsum_kernel,
        grid=(batch, seq // ts),
        in_specs=[pl.BlockSpec((1, ts, LANES), lambda b, s: (b, s, 0)),
                  pl.BlockSpec((1, LANES), lambda b, s: (0, 0)),
                  pl.BlockSpec((ts, ts), lambda b, s: (0, 0))],
        out_specs=pl.BlockSpec((1, ts, LANES), lambda b, s: (b, s, 0)),
        out_shape=jax.ShapeDtypeStruct((batch, seq, LANES), f32),
        scratch_shapes=[pltpu.VMEM((1, LANES), f32)],
        compiler_params=_params("parallel", "arbitrary"), name="fox_cumsum",
    )(f_logit.reshape(batch, seq, LANES), b_pad, tri)


def _head_mask(hh):
    lane = lax.broadcasted_iota(jnp.int32, (1, LANES), 1)
    return (lane >= hh * HEAD_DIM) & (lane < (hh + 1) * HEAD_DIM)


def _qk(qm, k2):
    return lax.dot_general(qm, k2, (((1,), (1,)), ((), ())), preferred_element_type=f32)


def _sb_attn_kernel(q_ref, k_ref, v_ref, u_ref, o_ref):
    qi = pl.program_id(2)
    tq = q_ref.shape[1]
    tk = tq
    q2 = q_ref[0]
    u = u_ref[...]
    row = lax.broadcasted_iota(jnp.int32, (tq, tk), 0)
    col = lax.broadcasted_iota(jnp.int32, (tq, tk), 1)
    strict = col < row

    def tile(qm, kb, run, acc, diag):
        k2 = k_ref[0, pl.ds(kb * tk, tk), :]
        v2 = v_ref[0, pl.ds(kb * tk, tk), :]
        z = _qk(qm, k2)
        sp = _softplus(z)
        log_not = jnp.where(strict, -sp, 0.0) if diag else -sp
        hi, lo = _split2(log_not)
        suffix = (jnp.dot(hi, u, preferred_element_type=f32)
                  + jnp.dot(lo, u, preferred_element_type=f32))
        a = jnp.exp((z - sp) + (suffix + run))
        if diag:
            a = jnp.where(strict, a, 0.0)
        acc = acc + jnp.dot(a.astype(bf16), v2, preferred_element_type=f32)
        run = run + suffix[:, 0:1] + log_not[:, 0:1]
        return run, acc

    out = jnp.zeros((tq, LANES), f32)
    for hh in range(HEADS_PER_STEP):
        mask = _head_mask(hh)
        qm = jnp.where(mask, q2, jnp.zeros_like(q2))
        run, acc = tile(qm, qi, jnp.zeros((tq, 1), f32), jnp.zeros((tq, LANES), f32), True)

        def body(i, carry, qm=qm):
            return tile(qm, qi - 1 - i, carry[0], carry[1], False)

        run, acc = lax.fori_loop(0, qi, body, (run, acc))
        out = jnp.where(mask, acc, out)
    o_ref[0] = out.astype(o_ref.dtype)


def _fox_attn_kernel(q_ref, k_ref, v_ref, cq_ref, ck_ref, o_ref):
    qi = pl.program_id(2)
    tq = q_ref.shape[1]
    tk = tq
    q2 = q_ref[0]
    row = lax.broadcasted_iota(jnp.int32, (tq, tk), 0)
    col = lax.broadcasted_iota(jnp.int32, (tq, tk), 1)
    causal = col <= row

    def tile(qm, cq, hh, kb, m, l, acc, diag):
        k2 = k_ref[0, pl.ds(kb * tk, tk), :]
        v2 = v_ref[0, pl.ds(kb * tk, tk), :]
        ck = ck_ref[0, hh, :, pl.ds(kb * tk, tk)]
        z = _qk(qm, k2) + (cq - ck)
        if diag:
            z = jnp.where(causal, z, -jnp.inf)
        m_new = jnp.maximum(m, jnp.max(z, axis=1, keepdims=True))
        alpha = jnp.exp(m - m_new)
        p = jnp.exp(z - m_new)
        l = alpha * l + jnp.sum(p, axis=1, keepdims=True)
        acc = alpha * acc + jnp.dot(p.astype(bf16), v2, preferred_element_type=f32)
        return m_new, l, acc

    out = jnp.zeros((tq, LANES), f32)
    for hh in range(HEADS_PER_STEP):
        mask = _head_mask(hh)
        qm = jnp.where(mask, q2, jnp.zeros_like(q2))
        cq = cq_ref[0, hh]
        init = (jnp.full((tq, 1), -jnp.inf, f32), jnp.zeros((tq, 1), f32), jnp.zeros((tq, LANES), f32))

        def body(kb, carry, qm=qm, cq=cq, hh=hh):
            return tile(qm, cq, hh, kb, *carry, False)

        m, l, acc = lax.fori_loop(0, qi, body, init)
        m, l, acc = tile(qm, cq, hh, qi, m, l, acc, True)
        out = jnp.where(mask, acc / l, out)
    o_ref[0] = out.astype(o_ref.dtype)


def attention(qkv, batch, seq, c=None):
    d = qkv.shape[1] // 3
    nlb = d // LANES
    tq = min(ATTN_BLOCK, seq)
    qkv3 = qkv.reshape(batch, seq, 3 * d)
    in_specs = [pl.BlockSpec((1, tq, LANES), lambda b, hp, qi: (b, qi, hp)),
                pl.BlockSpec((1, seq, LANES), lambda b, hp, qi: (b, 0, nlb + hp)),
                pl.BlockSpec((1, seq, LANES), lambda b, hp, qi: (b, 0, 2 * nlb + hp))]
    args = [qkv3, qkv3, qkv3]
    if c is None:
        kern = _sb_attn_kernel
        u = (jnp.arange(tq)[:, None] > jnp.arange(tq)[None, :]).astype(bf16)
        in_specs.append(pl.BlockSpec((tq, tq), lambda b, hp, qi: (0, 0)))
        args.append(u)
    else:
        kern = _fox_attn_kernel
        ch = jnp.transpose(c[:, :, :N_HEADS], (0, 2, 1))
        in_specs.append(pl.BlockSpec((1, HEADS_PER_STEP, tq, 1), lambda b, hp, qi: (b, hp, qi, 0)))
        in_specs.append(pl.BlockSpec((1, HEADS_PER_STEP, 1, seq), lambda b, hp, qi: (b, hp, 0, 0)))
        args += [ch[:, :, :, None], ch[:, :, None, :]]
    out = pl.pallas_call(
        kern, grid=(batch, nlb, seq // tq), in_specs=in_specs,
        out_specs=pl.BlockSpec((1, tq, LANES), lambda b, hp, qi: (b, qi, hp)),
        out_shape=jax.ShapeDtypeStruct((batch, seq, d), bf16),
        compiler_params=_params("parallel", "parallel", "arbitrary"), name="attention")(*args)
    return out.reshape(batch * seq, d)


def _oproj_router_kernel(o_ref, h_ref, wo_ref, g_ref, wrh_ref, wrl_ref, br_ref, tri_ref,
                         h1_ref, xn_ref, route_ref, cnt_ref, carry_ref):
    @pl.when(pl.program_id(0) == 0)
    def _():
        carry_ref[...] = jnp.zeros_like(carry_ref)

    tm = h_ref.shape[0]
    h1 = h_ref[...] + jnp.dot(o_ref[...], wo_ref[...], preferred_element_type=f32)
    h1_ref[...] = h1
    xn = _rmsnorm(h1, g_ref[...])
    xn_ref[...] = xn

    xh, xl = _split2(xn)
    logits = (jnp.dot(xh, wrh_ref[...], preferred_element_type=f32)
              + jnp.dot(xh, wrl_ref[...], preferred_element_type=f32)
              + jnp.dot(xl, wrh_ref[...], preferred_element_type=f32)) + br_ref[...]
    lane = lax.broadcasted_iota(jnp.int32, (tm, LANES), 1)
    lane_f = lane.astype(f32)
    work = jnp.where(lane < N_EXPERTS, logits, -jnp.inf)

    vals, idxs, hots = [], [], []
    for _ in range(TOP_K):
        mx = jnp.max(work, axis=1, keepdims=True)
        idx = jnp.min(jnp.where(work == mx, lane_f, float(LANES)), axis=1, keepdims=True)
        hot = lane_f == idx
        work = jnp.where(hot, -jnp.inf, work)
        vals.append(mx)
        idxs.append(idx)
        hots.append(hot)

    exps = [jnp.exp(v - vals[0]) for v in vals]
    denom = exps[0] + exps[1] + exps[2] + exps[3]
    gates = [e / denom for e in exps]

    sel = jnp.zeros((tm, LANES), f32)
    for hot in hots:
        sel = sel + hot.astype(f32)
    before = jnp.dot(tri_ref[...], sel.astype(bf16), preferred_element_type=f32) + carry_ref[...]
    ranks = [jnp.sum(jnp.where(hot, before, 0.0), axis=1, keepdims=True) for hot in hots]
    carry = carry_ref[...] + jnp.sum(sel, axis=0, keepdims=True)
    carry_ref[...] = carry
    cnt_ref[...] = jnp.broadcast_to(carry, cnt_ref.shape)

    route = jnp.zeros((tm, LANES), f32)
    for base, cols in ((ROUTE_IDX, idxs), (ROUTE_GATE, gates), (ROUTE_RANK, ranks)):
        for k, colv in enumerate(cols):
            route = jnp.where(lane == base + k, colv, route)
    route_ref[...] = route


def oproj_router(o, h, wo, g, wr_hi, wr_lo, br):
    n, d = h.shape
    tm = min(TOKEN_BLOCK, n)
    tri = (jnp.arange(tm)[None, :] < jnp.arange(tm)[:, None]).astype(bf16)
    row = lambda i: (i, 0)
    fixed = lambda i: (0, 0)
    return pl.pallas_call(
        _oproj_router_kernel,
        grid=(n // tm,),
        in_specs=[pl.BlockSpec((tm, d), row), pl.BlockSpec((tm, d), row),
                  pl.BlockSpec((d, d), fixed), pl.BlockSpec((1, d), fixed),
                  pl.BlockSpec((d, LANES), fixed), pl.BlockSpec((d, LANES), fixed),
                  pl.BlockSpec((1, LANES), fixed), pl.BlockSpec((tm, tm), fixed)],
        out_specs=[pl.BlockSpec((tm, d), row), pl.BlockSpec((tm, d), row),
                   pl.BlockSpec((tm, LANES), row), pl.BlockSpec((8, LANES), fixed)],
        out_shape=[jax.ShapeDtypeStruct((n, d), f32), jax.ShapeDtypeStruct((n, d), f32),
                   jax.ShapeDtypeStruct((n, LANES), f32), jax.ShapeDtypeStruct((8, LANES), f32)],
        scratch_shapes=[pltpu.VMEM((1, LANES), f32)],
        compiler_params=_params("arbitrary"), name="oproj_router",
    )(o, h, wo, g.reshape(1, d), wr_hi, wr_lo, br, tri)


def _dispatch_kernel(zlo_ref, zhi_ref, pos_ref, x_ref, rows_ref, zero_ref, sem):
    tm = x_ref.shape[0]

    def row_copy(src, r):
        return pltpu.make_async_copy(src, rows_ref.at[pl.ds(r, 1)], sem)

    @pl.when(pl.program_id(0) == 0)
    def _():
        zero_ref[...] = jnp.zeros_like(zero_ref)
        for wait in (False, True):
            def per_expert(e, _, wait=wait):
                def per_row(r, _):
                    cp = row_copy(zero_ref, r)
                    cp.wait() if wait else cp.start()
                    return 0
                return lax.fori_loop(zlo_ref[e], zhi_ref[e], per_row, 0)
            lax.fori_loop(0, N_EXPERTS, per_expert, 0)

    for wait in (False, True):
        def per_token(t, _, wait=wait):
            for k in range(TOP_K):
                cp = row_copy(x_ref.at[pl.ds(t, 1)], pos_ref[t * TOP_K + k])
                cp.wait() if wait else cp.start()
            return 0
        lax.fori_loop(0, tm, per_token, 0)


def dispatch(xn, pos_flat, zero_lo, zero_hi, n_rows):
    n, d = xn.shape
    tm = min(GATHER_BLOCK, n)
    grid_spec = pltpu.PrefetchScalarGridSpec(
        num_scalar_prefetch=2, grid=(n // tm,),
        in_specs=[pl.BlockSpec((tm * TOP_K,), lambda i, *_: (i,), memory_space=pltpu.SMEM),
                  pl.BlockSpec((tm, d), lambda i, *_: (i, 0))],
        out_specs=pl.BlockSpec(memory_space=pl.ANY),
        scratch_shapes=[pltpu.VMEM((1, d), f32), pltpu.SemaphoreType.DMA(())])
    return pl.pallas_call(
        _dispatch_kernel, grid_spec=grid_spec,
        out_shape=jax.ShapeDtypeStruct((n_rows, d), f32),
        compiler_params=_params("arbitrary"), name="dispatch",
    )(zero_lo, zero_hi, pos_flat, xn)


def _experts_kernel(be_ref, nused_ref, x_ref, wgu_ref, bgu_ref, wd_ref, bd_ref, y_ref):
    rb = pl.program_id(0)
    d_ff = wd_ref.shape[1]

    @pl.when(rb < nused_ref[0])
    def _():
        x = x_ref[...].astype(bf16)
        hgu = jnp.dot(x, wgu_ref[0], preferred_element_type=f32) + bgu_ref[0]
        glu = jnp.minimum(hgu[:, :d_ff], SWIGLU_LIMIT)
        lin = jnp.clip(hgu[:, d_ff:], -SWIGLU_LIMIT, SWIGLU_LIMIT)
        act = glu * (1.0 / (1.0 + jnp.exp(-SWIGLU_ALPHA * glu))) * (lin + 1.0)
        y_ref[...] = jnp.dot(act.astype(bf16), wd_ref[0], preferred_element_type=f32) + bd_ref[0]

    @pl.when(rb >= nused_ref[0])
    def _():
        y_ref[...] = jnp.zeros_like(y_ref)


def experts(x_rows, block_expert, n_used, w_gu, b_gu, w_down, b_down):
    n_rows, d = x_rows.shape
    e, _, f2 = w_gu.shape
    d_ff = f2 // 2
    grid_spec = pltpu.PrefetchScalarGridSpec(
        num_scalar_prefetch=2, grid=(n_rows // ROW_BLOCK,),
        in_specs=[pl.BlockSpec((ROW_BLOCK, d), lambda rb, be, nu: (rb, 0)),
                  pl.BlockSpec((1, d, f2), lambda rb, be, nu: (be[rb], 0, 0)),
                  pl.BlockSpec((1, 1, f2), lambda rb, be, nu: (be[rb], 0, 0)),
                  pl.BlockSpec((1, d_ff, d), lambda rb, be, nu: (be[rb], 0, 0)),
                  pl.BlockSpec((1, 1, d), lambda rb, be, nu: (be[rb], 0, 0))],
        out_specs=pl.BlockSpec((ROW_BLOCK, d), lambda rb, be, nu: (rb, 0)))
    return pl.pallas_call(
        _experts_kernel, grid_spec=grid_spec,
        out_shape=jax.ShapeDtypeStruct((n_rows, d), f32),
        compiler_params=_params("arbitrary"), name="experts",
    )(block_expert, n_used, x_rows, w_gu, b_gu.reshape(e, 1, f2), w_down, b_down.reshape(e, 1, d))


def _combine_ple_kernel(pos_ref, route_ref, h_ref, p_ref, g_ref, wg_ref, wp_ref, gf_ref, yrows_ref,
                        o_ref, ybuf_ref, sem, *, final):
    tm = h_ref.shape[0]

    for wait in (False, True):
        if wait:
            proj = jnp.dot(p_ref[...].astype(bf16), wp_ref[...], preferred_element_type=f32)

        def per_token(t, _, wait=wait):
            for k in range(TOP_K):
                cp = pltpu.make_async_copy(yrows_ref.at[pl.ds(pos_ref[t * TOP_K + k], 1)],
                                           ybuf_ref.at[k, pl.ds(t, 1)], sem)
                cp.wait() if wait else cp.start()
            return 0
        lax.fori_loop(0, tm, per_token, 0)

    route = route_ref[...]
    h2 = h_ref[...]
    for k in range(TOP_K):
        h2 = h2 + route[:, ROUTE_GATE + k:ROUTE_GATE + k + 1] * ybuf_ref[k]
    hn = _rmsnorm(h2, g_ref[...]).astype(bf16)
    gate = 1.0 / (1.0 + jnp.exp(-jnp.dot(hn, wg_ref[...], preferred_element_type=f32)))
    h3 = h2 + gate * proj
    if final:
        h3 = _rmsnorm(h3, gf_ref[...])
    o_ref[...] = h3


def combine_ple(pos_flat, route, h1, y_rows, p, g, wg, wp, g_final, final):
    n, d = h1.shape
    pd = p.shape[1]
    tm = min(GATHER_BLOCK, n)
    row = lambda i: (i, 0)
    fixed = lambda i: (0, 0)
    return pl.pallas_call(
        functools.partial(_combine_ple_kernel, final=final),
        grid=(n // tm,),
        in_specs=[pl.BlockSpec((tm * TOP_K,), lambda i: (i,), memory_space=pltpu.SMEM),
                  pl.BlockSpec((tm, LANES), row), pl.BlockSpec((tm, d), row),
                  pl.BlockSpec((tm, pd), row), pl.BlockSpec((1, d), fixed),
                  pl.BlockSpec((d, d), fixed), pl.BlockSpec((pd, d), fixed),
                  pl.BlockSpec((1, d), fixed), pl.BlockSpec(memory_space=pl.ANY)],
        out_specs=pl.BlockSpec((tm, d), row),
        out_shape=jax.ShapeDtypeStruct((n, d), f32),
        scratch_shapes=[pltpu.VMEM((TOP_K, tm, d), f32), pltpu.SemaphoreType.DMA(())],
        compiler_params=_params("arbitrary"), name="combine_ple",
    )(pos_flat, route, h1, p, g.reshape(1, d), wg, wp, g_final.reshape(1, d), y_rows)


def _routing_tables(route, counts_f, n):
    counts = counts_f[0, :N_EXPERTS].astype(jnp.int32)
    padded = (counts + ROW_BLOCK - 1) // ROW_BLOCK * ROW_BLOCK
    pad_end = jnp.cumsum(padded)
    pad_start = pad_end - padded
    n_blocks = -(-(n * TOP_K + N_EXPERTS * (ROW_BLOCK - 1)) // ROW_BLOCK)
    n_rows = n_blocks * ROW_BLOCK
    idx = route[:, ROUTE_IDX:ROUTE_IDX + TOP_K].astype(jnp.int32)
    rank = route[:, ROUTE_RANK:ROUTE_RANK + TOP_K].astype(jnp.int32)
    pos_flat = (pad_start[idx] + rank).reshape(-1)
    block_start = jnp.arange(n_blocks, dtype=jnp.int32) * ROW_BLOCK
    block_expert = jnp.minimum(jnp.searchsorted(pad_end, block_start, side='right'),
                               N_EXPERTS - 1).astype(jnp.int32)
    n_used = (pad_end[-1:] // ROW_BLOCK).astype(jnp.int32)
    zero_lo = (pad_start + counts).astype(jnp.int32)
    zero_hi = jnp.concatenate([pad_start[1:], jnp.array([n_rows], jnp.int32)]).astype(jnp.int32)
    return pos_flat, block_expert, n_used, zero_lo, zero_hi, n_rows


def kernel(x, p, g_mix, w_qkv, w_fgate, b_fgate, w_o, g_moe, w_router, b_router, w_gu, b_gu,
           w_down, b_down, g_ple, w_ple_gate, w_ple_proj, g_final):
    batch, seq, d = x.shape
    depth = w_qkv.shape[0]
    n = batch * seq
    h = x.reshape(n, d)
    q_scale = jnp.concatenate([jnp.full((d,), HEAD_DIM ** -0.5, f32), jnp.ones((2 * d,), f32)])
    for i in range(depth):
        wqkv = (w_qkv[i] * q_scale).astype(bf16)
        if i % 2 == 0:
            qkv = norm_proj(h, g_mix[i], wqkv)
            o = attention(qkv, batch, seq)
        else:
            j = i // 2
            wf = jnp.pad(w_fgate[j], ((0, 0), (0, LANES - N_HEADS))).astype(bf16)
            bf = jnp.pad(b_fgate[j], (0, LANES - N_HEADS)).reshape(1, LANES)
            qkv, f_logit = norm_proj(h, g_mix[i], wqkv, wf)
            c = fox_cumsum(f_logit, bf, batch, seq)
            o = attention(qkv, batch, seq, c)

        wr = jnp.pad(w_router[i], ((0, 0), (0, LANES - N_EXPERTS)))
        wr_hi = wr.astype(bf16)
        wr_lo = (wr - wr_hi.astype(f32)).astype(bf16)
        br = jnp.pad(b_router[i], (0, LANES - N_EXPERTS)).reshape(1, LANES)
        h1, xn, route, counts = oproj_router(o, h, w_o[i].astype(bf16), g_moe[i], wr_hi, wr_lo, br)

        pos_flat, block_expert, n_used, zero_lo, zero_hi, n_rows = _routing_tables(route, counts, n)
        x_rows = dispatch(xn, pos_flat, zero_lo, zero_hi, n_rows)
        y_rows = experts(x_rows, block_expert, n_used, w_gu[i].astype(bf16), b_gu[i],
                         w_down[i].astype(bf16), b_down[i])
        h = combine_ple(pos_flat, route, h1, y_rows, p[i].reshape(n, -1), g_ple[i],
                        w_ple_gate[i].astype(bf16), w_ple_proj[i].astype(bf16), g_final,
                        final=(i == depth - 1))
    return h.reshape(batch, seq, d)
```

```python
import functools

import jax
import jax.numpy as jnp
from jax import lax
from jax.experimental import pallas as pl
from jax.experimental.pallas import tpu as pltpu

N_HEADS = 16
HEAD_DIM = 64
N_EXPERTS = 32
TOP_K = 4
SWIGLU_LIMIT = 7.0
SWIGLU_ALPHA = 1.702
RMS_EPS = 1e-6

LANES = 128
HEADS_PER_STEP = LANES // HEAD_DIM
VMEM_LIMIT_BYTES = 48 * 1024 * 1024

ROW_BLOCK = 256
ATTN_BLOCK = 256
TOKEN_BLOCK = 512
GATHER_BLOCK = 256

SB_SKIP_BELOW = -104.0
FOX_KEY_BLOCK = 512

ROUTE_IDX, ROUTE_GATE, ROUTE_RANK = 0, TOP_K, 2 * TOP_K

f32 = jnp.float32
bf16 = jnp.bfloat16


def _params(*semantics):
    return pltpu.CompilerParams(dimension_semantics=semantics, vmem_limit_bytes=VMEM_LIMIT_BYTES)


def _rmsnorm(x, g):
    var = jnp.mean(x * x, axis=-1, keepdims=True)
    return x * lax.rsqrt(var + RMS_EPS) * g


def _neg_abs(x):
    bits = lax.bitcast_convert_type(x, jnp.uint32) | jnp.uint32(0x80000000)
    return lax.bitcast_convert_type(bits, f32)


def _split2(x):
    hi = x.astype(bf16)
    lo = (x - hi.astype(f32)).astype(bf16)
    return hi, lo


def _norm_proj_kernel(h_ref, g_ref, w_ref, *rest, with_gate):
    xn = _rmsnorm(h_ref[...], g_ref[...]).astype(bf16)
    if with_gate:
        wf_ref, o_ref, of_ref = rest
        of_ref[...] = jnp.dot(xn, wf_ref[...], preferred_element_type=f32)
    else:
        (o_ref,) = rest
    o_ref[...] = jnp.dot(xn, w_ref[...], preferred_element_type=f32).astype(o_ref.dtype)


def norm_proj(h, g, w, wf=None):
    n, d = h.shape
    m = w.shape[1]
    tm = min(TOKEN_BLOCK, n)
    with_gate = wf is not None
    in_specs = [pl.BlockSpec((tm, d), lambda i: (i, 0)),
                pl.BlockSpec((1, d), lambda i: (0, 0)),
                pl.BlockSpec((d, m), lambda i: (0, 0))]
    out_specs = [pl.BlockSpec((tm, m), lambda i: (i, 0))]
    out_shape = [jax.ShapeDtypeStruct((n, m), bf16)]
    args = [h, g.reshape(1, d), w]
    if with_gate:
        in_specs.append(pl.BlockSpec((d, LANES), lambda i: (0, 0)))
        out_specs.append(pl.BlockSpec((tm, LANES), lambda i: (i, 0)))
        out_shape.append(jax.ShapeDtypeStruct((n, LANES), f32))
        args.append(wf)
    out = pl.pallas_call(
        functools.partial(_norm_proj_kernel, with_gate=with_gate),
        grid=(n // tm,), in_specs=in_specs, out_specs=out_specs, out_shape=out_shape,
        compiler_params=_params("parallel"), name="norm_proj")(*args)
    return out if with_gate else out[0]


def _fox_cumsum_kernel(f_ref, b_ref, tri_ref, c_ref, carry_ref):
    @pl.when(pl.program_id(1) == 0)
    def _():
        carry_ref[...] = jnp.zeros_like(carry_ref)

    x = f_ref[0] + b_ref[...]
    log_f = jnp.minimum(x, 0.0) - jnp.log(1.0 + jnp.exp(-jnp.abs(x)))
    p1 = log_f.astype(bf16)
    r1 = log_f - p1.astype(f32)
    p2 = r1.astype(bf16)
    p3 = (r1 - p2.astype(f32)).astype(bf16)
    tri = tri_ref[...]
    c = (jnp.dot(tri, p1, preferred_element_type=f32)
         + jnp.dot(tri, p2, preferred_element_type=f32)
         + jnp.dot(tri, p3, preferred_element_type=f32)) + carry_ref[...]
    c_ref[0] = c
    carry_ref[...] = c[-1:, :]


def fox_cumsum(f_logit, b_pad, batch, seq):
    ts = min(ATTN_BLOCK, seq)
    tri = (jnp.arange(ts)[None, :] <= jnp.arange(ts)[:, None]).astype(bf16)
    return pl.pallas_call(
        _fox_cumsum_kernel,
        grid=(batch, seq // ts),
        in_specs=[pl.BlockSpec((1, ts, LANES), lambda b, s: (b, s, 0)),
                  pl.BlockSpec((1, LANES), lambda b, s: (0, 0)),
                  pl.BlockSpec((ts, ts), lambda b, s: (0, 0))],
        out_specs=pl.BlockSpec((1, ts, LANES), lambda b, s: (b, s, 0)),
        out_shape=jax.ShapeDtypeStruct((batch, seq, LANES), f32),
        scratch_shapes=[pltpu.VMEM((1, LANES), f32)],
        compiler_params=_params("parallel", "arbitrary"), name="fox_cumsum",
    )(f_logit.reshape(batch, seq, LANES), b_pad, tri)


def _head_mask(hh):
    lane = lax.broadcasted_iota(jnp.int32, (1, LANES), 1)
    return (lane >= hh * HEAD_DIM) & (lane < (hh + 1) * HEAD_DIM)


def _qk(qm, k2):
    return lax.dot_general(qm, k2, (((1,), (1,)), ((), ())), preferred_element_type=f32)


def _sb_attn_kernel(q_ref, k_ref, v_ref, u_ref, o_ref):
    qi = pl.program_id(2)
    tq = q_ref.shape[1]
    tk = tq
    q2 = q_ref[0]
    u2 = u_ref[...]
    masks = [_head_mask(hh) for hh in range(HEADS_PER_STEP)]
    neg_qs = [jnp.where(m, -q2, jnp.zeros_like(q2)) for m in masks]
    heads = range(HEADS_PER_STEP)

    def two_blocks(kb, runs, acc, diag):
        has_next = kb >= 1
        kbs = [kb, jnp.maximum(kb - 1, 0)]
        k2s = [k_ref[0, pl.ds(b * tk, tk), :] for b in kbs]
        v2s = [v_ref[0, pl.ds(b * tk, tk), :] for b in kbs]
        vv = jnp.concatenate([jnp.where(m, v2, jnp.zeros_like(v2)) for v2 in v2s for m in masks], axis=0)
        if diag:
            strict = (lax.broadcasted_iota(jnp.int32, (tq, tk), 1)
                      < lax.broadcasted_iota(jnp.int32, (tq, tk), 0))
        chains = [(blk, hh) for blk in range(2) for hh in heads]
        nzs = {c: _qk(neg_qs[c[1]], k2s[c[0]]) for c in chains}
        suffixes = {}
        for c in chains:
            nz = nzs[c]
            log_not = jnp.minimum(nz, 0.0) - jnp.log(1.0 + jnp.exp(_neg_abs(nz)))
            if diag and c[0] == 0:
                log_not = jnp.where(strict, log_not, 0.0)
            hi, lo = _split2(log_not)
            suffixes[c] = jnp.dot(jnp.concatenate([hi, lo], axis=1), u2, preferred_element_type=f32)
        weights, new_runs = [], []
        for blk in range(2):
            for hh in heads:
                if blk == 0:
                    run = runs[hh]
                else:
                    after = runs[hh] + suffixes[(0, hh)][:, 0:1]
                    run = jnp.where(has_next, after, -jnp.inf)
                    new_runs.append(jnp.where(has_next, after + suffixes[(1, hh)][:, 0:1], after))
                a = jnp.exp((run - nzs[(blk, hh)]) + suffixes[(blk, hh)])
                if diag and blk == 0:
                    a = jnp.where(strict, a, 0.0)
                weights.append(a.astype(bf16))
        acc = acc + jnp.dot(jnp.concatenate(weights, axis=1), vv, preferred_element_type=f32)
        return new_runs, acc

    def alive(runs):
        return (jnp.max(jnp.maximum(runs[0], runs[1])) > SB_SKIP_BELOW).astype(jnp.int32)

    zero_run = jnp.zeros((tq, 1), f32)
    runs, acc = two_blocks(qi, [zero_run, zero_run], jnp.zeros((tq, LANES), f32), True)

    def cond(c):
        return (c[0] >= 0) & (c[1] > 0)

    def body(c):
        kb, _, r0, r1, acc = c
        (r0, r1), acc = two_blocks(kb, [r0, r1], acc, False)
        return kb - 2, alive([r0, r1]), r0, r1, acc

    acc = lax.while_loop(cond, body, (qi - 2, alive(runs), runs[0], runs[1], acc))[4]
    o_ref[0] = acc.astype(o_ref.dtype)


def _fox_attn_kernel(q_ref, k_ref, v_ref, cq_ref, ck_ref, o_ref):
    hp = pl.program_id(1)
    qi = pl.program_id(2)
    tq = q_ref.shape[1]
    tk = min(FOX_KEY_BLOCK, k_ref.shape[1])
    q2 = q_ref[0]
    masks = [_head_mask(hh) for hh in range(HEADS_PER_STEP)]
    qms = [jnp.where(m, q2, jnp.zeros_like(q2)) for m in masks]
    n_full = (qi * tq + 1) // tk
    row = lax.broadcasted_iota(jnp.int32, (tq, tk), 0) + qi * tq
    col = lax.broadcasted_iota(jnp.int32, (tq, tk), 1) + n_full * tk
    causal = col <= row
    lane = lax.broadcasted_iota(jnp.int32, (1, LANES), 1)
    c_blk = cq_ref[0]
    cqs = [jnp.sum(jnp.where(lane == hp * HEADS_PER_STEP + hh, c_blk, 0.0), axis=1, keepdims=True)
           for hh in range(HEADS_PER_STEP)]

    def tile_pair(kb, ms, ls, acc, diag):
        k2 = k_ref[0, pl.ds(kb * tk, tk), :]
        v2 = v_ref[0, pl.ds(kb * tk, tk), :]
        probs, alphas, new_ms, new_ls = [], [], [], []
        for hh in range(HEADS_PER_STEP):
            ck = ck_ref[0, pl.ds(hp * HEADS_PER_STEP + hh, 1), pl.ds(kb * tk, tk)]
            w = _qk(qms[hh], k2) - ck
            if diag:
                w = jnp.where(causal, w, -jnp.inf)
            m_new = jnp.maximum(ms[hh], cqs[hh] + jnp.max(w, axis=1, keepdims=True))
            p = jnp.exp(w - (m_new - cqs[hh]))
            alpha = jnp.exp(ms[hh] - m_new)
            probs.append(p.astype(bf16))
            alphas.append(alpha)
            new_ms.append(m_new)
            new_ls.append(alpha * ls[hh] + jnp.sum(p, axis=1, keepdims=True))
        vv = jnp.concatenate([jnp.where(m, v2, jnp.zeros_like(v2)) for m in masks], axis=0)
        acc = (acc * jnp.where(masks[0], alphas[0], alphas[1])
               + jnp.dot(jnp.concatenate(probs, axis=1), vv, preferred_element_type=f32))
        return new_ms, new_ls, acc

    neg_inf = jnp.full((tq, 1), -jnp.inf, f32)
    zero = jnp.zeros((tq, 1), f32)

    def body(kb, c):
        ms, ls, acc = tile_pair(kb, [c[0], c[1]], [c[2], c[3]], c[4], False)
        return ms[0], ms[1], ls[0], ls[1], acc

    c = lax.fori_loop(0, n_full, body, (neg_inf, neg_inf, zero, zero, jnp.zeros((tq, LANES), f32)))
    ms, ls, acc = tile_pair(n_full, [c[0], c[1]], [c[2], c[3]], c[4], True)
    o_ref[0] = (acc / jnp.where(masks[0], ls[0], ls[1])).astype(o_ref.dtype)


def attention(qkv, batch, seq, c=None):
    d = qkv.shape[1] // 3
    nlb = d // LANES
    tq = min(ATTN_BLOCK, seq)
    qkv3 = qkv.reshape(batch, seq, 3 * d)
    in_specs = [pl.BlockSpec((1, tq, LANES), lambda b, hp, qi: (b, qi, hp)),
                pl.BlockSpec((1, seq, LANES), lambda b, hp, qi: (b, 0, nlb + hp)),
                pl.BlockSpec((1, seq, LANES), lambda b, hp, qi: (b, 0, 2 * nlb + hp))]
    args = [qkv3, qkv3, qkv3]
    if c is None:
        kern = _sb_attn_kernel
        u = (jnp.arange(tq)[:, None] >= jnp.arange(tq)[None, :]).astype(bf16)
        in_specs.append(pl.BlockSpec((2 * tq, tq), lambda b, hp, qi: (0, 0)))
        args.append(jnp.concatenate([u, u], axis=0))
    else:
        kern = _fox_attn_kernel
        ch = jnp.transpose(c[:, :, :N_HEADS], (0, 2, 1))
        in_specs.append(pl.BlockSpec((1, tq, LANES), lambda b, hp, qi: (b, qi, 0)))
        in_specs.append(pl.BlockSpec((1, N_HEADS, seq), lambda b, hp, qi: (b, 0, 0)))
        args += [c, ch]
    out = pl.pallas_call(
        kern, grid=(batch, nlb, seq // tq), in_specs=in_specs,
        out_specs=pl.BlockSpec((1, tq, LANES), lambda b, hp, qi: (b, qi, hp)),
        out_shape=jax.ShapeDtypeStruct((batch, seq, d), bf16),
        compiler_params=_params("parallel", "parallel", "arbitrary"), name="attention")(*args)
    return out.reshape(batch * seq, d)


def _oproj_router_kernel(o_ref, h_ref, wo_ref, g_ref, wrh_ref, wrl_ref, br_ref, tri_ref,
                         h1_ref, xn_ref, route_ref, cnt_ref, carry_ref):
    @pl.when(pl.program_id(0) == 0)
    def _():
        carry_ref[...] = jnp.zeros_like(carry_ref)

    tm = h_ref.shape[0]
    h1 = h_ref[...] + jnp.dot(o_ref[...], wo_ref[...], preferred_element_type=f32)
    h1_ref[...] = h1
    xn = _rmsnorm(h1, g_ref[...])
    xn_ref[...] = xn

    xh, xl = _split2(xn)
    logits = (jnp.dot(xh, wrh_ref[...], preferred_element_type=f32)
              + jnp.dot(xh, wrl_ref[...], preferred_element_type=f32)
              + jnp.dot(xl, wrh_ref[...], preferred_element_type=f32)) + br_ref[...]
    lane = lax.broadcasted_iota(jnp.int32, (tm, LANES), 1)
    lane_f = lane.astype(f32)
    work = jnp.where(lane < N_EXPERTS, logits, -jnp.inf)

    vals, idxs, hots = [], [], []
    for _ in range(TOP_K):
        mx = jnp.max(work, axis=1, keepdims=True)
        idx = jnp.min(jnp.where(work == mx, lane_f, float(LANES)), axis=1, keepdims=True)
        hot = lane_f == idx
        work = jnp.where(hot, -jnp.inf, work)
        vals.append(mx)
        idxs.append(idx)
        hots.append(hot)

    exps = [jnp.exp(v - vals[0]) for v in vals]
    denom = exps[0] + exps[1] + exps[2] + exps[3]
    gates = [e / denom for e in exps]

    sel = jnp.zeros((tm, LANES), f32)
    for hot in hots:
        sel = sel + hot.astype(f32)
    before = jnp.dot(tri_ref[...], sel.astype(bf16), preferred_element_type=f32) + carry_ref[...]
    ranks = [jnp.sum(jnp.where(hot, before, 0.0), axis=1, keepdims=True) for hot in hots]
    carry = carry_ref[...] + jnp.sum(sel, axis=0, keepdims=True)
    carry_ref[...] = carry
    cnt_ref[...] = jnp.broadcast_to(carry, cnt_ref.shape)

    route = jnp.zeros((tm, LANES), f32)
    for base, cols in ((ROUTE_IDX, idxs), (ROUTE_GATE, gates), (ROUTE_RANK, ranks)):
        for k, colv in enumerate(cols):
            route = jnp.where(lane == base + k, colv, route)
    route_ref[...] = route


def oproj_router(o, h, wo, g, wr_hi, wr_lo, br):
    n, d = h.shape
    tm = min(TOKEN_BLOCK, n)
    tri = (jnp.arange(tm)[None, :] < jnp.arange(tm)[:, None]).astype(bf16)
    row = lambda i: (i, 0)
    fixed = lambda i: (0, 0)
    return pl.pallas_call(
        _oproj_router_kernel,
        grid=(n // tm,),
        in_specs=[pl.BlockSpec((tm, d), row), pl.BlockSpec((tm, d), row),
                  pl.BlockSpec((d, d), fixed), pl.BlockSpec((1, d), fixed),
                  pl.BlockSpec((d, LANES), fixed), pl.BlockSpec((d, LANES), fixed),
                  pl.BlockSpec((1, LANES), fixed), pl.BlockSpec((tm, tm), fixed)],
        out_specs=[pl.BlockSpec((tm, d), row), pl.BlockSpec((tm, d), row),
                   pl.BlockSpec((tm, LANES), row), pl.BlockSpec((8, LANES), fixed)],
        out_shape=[jax.ShapeDtypeStruct((n, d), f32), jax.ShapeDtypeStruct((n, d), f32),
                   jax.ShapeDtypeStruct((n, LANES), f32), jax.ShapeDtypeStruct((8, LANES), f32)],
        scratch_shapes=[pltpu.VMEM((1, LANES), f32)],
        compiler_params=_params("arbitrary"), name="oproj_router",
    )(o, h, wo, g.reshape(1, d), wr_hi, wr_lo, br, tri)


def _dispatch_kernel(zlo_ref, zhi_ref, pos_ref, x_ref, rows_ref, zero_ref, sem):
    tm = x_ref.shape[0]

    def row_copy(src, r):
        return pltpu.make_async_copy(src, rows_ref.at[pl.ds(r, 1)], sem)

    @pl.when(pl.program_id(0) == 0)
    def _():
        zero_ref[...] = jnp.zeros_like(zero_ref)
        for wait in (False, True):
            def per_expert(e, _, wait=wait):
                def per_row(r, _):
                    cp = row_copy(zero_ref, r)
                    cp.wait() if wait else cp.start()
                    return 0
                return lax.fori_loop(zlo_ref[e], zhi_ref[e], per_row, 0)
            lax.fori_loop(0, N_EXPERTS, per_expert, 0)

    for wait in (False, True):
        def per_token(t, _, wait=wait):
            for k in range(TOP_K):
                cp = row_copy(x_ref.at[pl.ds(t, 1)], pos_ref[t * TOP_K + k])
                cp.wait() if wait else cp.start()
            return 0
        lax.fori_loop(0, tm, per_token, 0)


def dispatch(xn, pos_flat, zero_lo, zero_hi, n_rows):
    n, d = xn.shape
    tm = min(GATHER_BLOCK, n)
    grid_spec = pltpu.PrefetchScalarGridSpec(
        num_scalar_prefetch=2, grid=(n // tm,),
        in_specs=[pl.BlockSpec((tm * TOP_K,), lambda i, *_: (i,), memory_space=pltpu.SMEM),
                  pl.BlockSpec((tm, d), lambda i, *_: (i, 0))],
        out_specs=pl.BlockSpec(memory_space=pl.ANY),
        scratch_shapes=[pltpu.VMEM((1, d), f32), pltpu.SemaphoreType.DMA(())])
    return pl.pallas_call(
        _dispatch_kernel, grid_spec=grid_spec,
        out_shape=jax.ShapeDtypeStruct((n_rows, d), f32),
        compiler_params=_params("arbitrary"), name="dispatch",
    )(zero_lo, zero_hi, pos_flat, xn)


def _experts_kernel(be_ref, nused_ref, x_ref, wgu_ref, bgu_ref, wd_ref, bd_ref, y_ref):
    rb = pl.program_id(0)
    d_ff = wd_ref.shape[2]

    @pl.when(rb < nused_ref[0])
    def _():
        x = x_ref[...].astype(bf16)
        hgu = jnp.dot(x, wgu_ref[0, 0], preferred_element_type=f32) + bgu_ref[0, 0]
        glu = jnp.minimum(hgu[:, :d_ff], SWIGLU_LIMIT)
        lin = jnp.clip(hgu[:, d_ff:], -SWIGLU_LIMIT, SWIGLU_LIMIT)
        act = glu * (1.0 / (1.0 + jnp.exp(-SWIGLU_ALPHA * glu))) * (lin + 1.0)
        y_ref[...] = jnp.dot(act.astype(bf16), wd_ref[0, 0], preferred_element_type=f32) + bd_ref[0, 0]

    @pl.when(rb >= nused_ref[0])
    def _():
        y_ref[...] = jnp.zeros_like(y_ref)


def experts(x_rows, block_expert, n_used, layer, w_gu, b_gu, w_down, b_down):
    n_rows, d = x_rows.shape
    f2 = w_gu.shape[3]
    d_ff = f2 // 2
    wmap = lambda rb, be, nu: (layer, be[rb], 0, 0)
    grid_spec = pltpu.PrefetchScalarGridSpec(
        num_scalar_prefetch=2, grid=(n_rows // ROW_BLOCK,),
        in_specs=[pl.BlockSpec((ROW_BLOCK, d), lambda rb, be, nu: (rb, 0)),
                  pl.BlockSpec((1, 1, d, f2), wmap), pl.BlockSpec((1, 1, 1, f2), wmap),
                  pl.BlockSpec((1, 1, d_ff, d), wmap), pl.BlockSpec((1, 1, 1, d), wmap)],
        out_specs=pl.BlockSpec((ROW_BLOCK, d), lambda rb, be, nu: (rb, 0)))
    return pl.pallas_call(
        _experts_kernel, grid_spec=grid_spec,
        out_shape=jax.ShapeDtypeStruct((n_rows, d), f32),
        compiler_params=_params("arbitrary"), name="experts",
    )(block_expert, n_used, x_rows, w_gu, b_gu, w_down, b_down)


def _combine_ple_kernel(pos_ref, route_ref, h_ref, p_ref, g_ref, wg_ref, wp_ref, gf_ref, yrows_ref,
                        o_ref, ybuf_ref, sem, *, final):
    tm = h_ref.shape[0]

    for wait in (False, True):
        if wait:
            proj = jnp.dot(p_ref[...].astype(bf16), wp_ref[...], preferred_element_type=f32)

        def per_token(t, _, wait=wait):
            for k in range(TOP_K):
                cp = pltpu.make_async_copy(yrows_ref.at[pl.ds(pos_ref[t * TOP_K + k], 1)],
                                           ybuf_ref.at[k, pl.ds(t, 1)], sem)
                cp.wait() if wait else cp.start()
            return 0
        lax.fori_loop(0, tm, per_token, 0)

    route = route_ref[...]
    h2 = h_ref[...]
    for k in range(TOP_K):
        h2 = h2 + route[:, ROUTE_GATE + k:ROUTE_GATE + k + 1] * ybuf_ref[k]
    hn = _rmsnorm(h2, g_ref[...]).astype(bf16)
    gate = 1.0 / (1.0 + jnp.exp(-jnp.dot(hn, wg_ref[...], preferred_element_type=f32)))
    h3 = h2 + gate * proj
    if final:
        h3 = _rmsnorm(h3, gf_ref[...])
    o_ref[...] = h3


def combine_ple(pos_flat, route, h1, y_rows, p, g, wg, wp, g_final, final):
    n, d = h1.shape
    pd = p.shape[1]
    tm = min(GATHER_BLOCK, n)
    row = lambda i: (i, 0)
    fixed = lambda i: (0, 0)
    return pl.pallas_call(
        functools.partial(_combine_ple_kernel, final=final),
        grid=(n // tm,),
        in_specs=[pl.BlockSpec((tm * TOP_K,), lambda i: (i,), memory_space=pltpu.SMEM),
                  pl.BlockSpec((tm, LANES), row), pl.BlockSpec((tm, d), row),
                  pl.BlockSpec((tm, pd), row), pl.BlockSpec((1, d), fixed),
                  pl.BlockSpec((d, d), fixed), pl.BlockSpec((pd, d), fixed),
                  pl.BlockSpec((1, d), fixed), pl.BlockSpec(memory_space=pl.ANY)],
        out_specs=pl.BlockSpec((tm, d), row),
        out_shape=jax.ShapeDtypeStruct((n, d), f32),
        scratch_shapes=[pltpu.VMEM((TOP_K, tm, d), f32), pltpu.SemaphoreType.DMA(())],
        compiler_params=_params("arbitrary"), name="combine_ple",
    )(pos_flat, route, h1, p, g.reshape(1, d), wg, wp, g_final.reshape(1, d), y_rows)


def _routing_tables(route, counts_f, n):
    counts = counts_f[0, :N_EXPERTS].astype(jnp.int32)
    padded = (counts + ROW_BLOCK - 1) // ROW_BLOCK * ROW_BLOCK
    pad_end = jnp.cumsum(padded)
    pad_start = pad_end - padded
    n_blocks = -(-(n * TOP_K + N_EXPERTS * (ROW_BLOCK - 1)) // ROW_BLOCK)
    n_rows = n_blocks * ROW_BLOCK
    idx = route[:, ROUTE_IDX:ROUTE_IDX + TOP_K].astype(jnp.int32)
    rank = route[:, ROUTE_RANK:ROUTE_RANK + TOP_K].astype(jnp.int32)
    pos_flat = (pad_start[idx] + rank).reshape(-1)
    block_start = jnp.arange(n_blocks, dtype=jnp.int32) * ROW_BLOCK
    block_expert = jnp.minimum(jnp.sum(block_start[:, None] >= pad_end[None, :], axis=1),
                               N_EXPERTS - 1).astype(jnp.int32)
    n_used = (pad_end[-1:] // ROW_BLOCK).astype(jnp.int32)
    zero_lo = (pad_start + counts).astype(jnp.int32)
    zero_hi = jnp.concatenate([pad_start[1:], jnp.array([n_rows], jnp.int32)]).astype(jnp.int32)
    return pos_flat, block_expert, n_used, zero_lo, zero_hi, n_rows


def kernel(x, p, g_mix, w_qkv, w_fgate, b_fgate, w_o, g_moe, w_router, b_router, w_gu, b_gu,
           w_down, b_down, g_ple, w_ple_gate, w_ple_proj, g_final):
    batch, seq, d = x.shape
    depth = w_qkv.shape[0]
    n = batch * seq
    h = x.reshape(n, d)
    q_scale = jnp.concatenate([jnp.full((d,), HEAD_DIM ** -0.5, f32), jnp.ones((2 * d,), f32)])
    n_exp = w_gu.shape[1]
    w_gu_b, w_down_b = w_gu.astype(bf16), w_down.astype(bf16)
    b_gu4 = b_gu.reshape(depth, n_exp, 1, -1)
    b_down4 = b_down.reshape(depth, n_exp, 1, -1)
    for i in range(depth):
        wqkv = (w_qkv[i] * q_scale).astype(bf16)
        if i % 2 == 0:
            qkv = norm_proj(h, g_mix[i], wqkv)
            o = attention(qkv, batch, seq)
        else:
            j = i // 2
            wf = jnp.pad(w_fgate[j], ((0, 0), (0, LANES - N_HEADS))).astype(bf16)
            bf = jnp.pad(b_fgate[j], (0, LANES - N_HEADS)).reshape(1, LANES)
            qkv, f_logit = norm_proj(h, g_mix[i], wqkv, wf)
            c = fox_cumsum(f_logit, bf, batch, seq)
            o = attention(qkv, batch, seq, c)

        wr = jnp.pad(w_router[i], ((0, 0), (0, LANES - N_EXPERTS)))
        wr_hi = wr.astype(bf16)
        wr_lo = (wr - wr_hi.astype(f32)).astype(bf16)
        br = jnp.pad(b_router[i], (0, LANES - N_EXPERTS)).reshape(1, LANES)
        h1, xn, route, counts = oproj_router(o, h, w_o[i].astype(bf16), g_moe[i], wr_hi, wr_lo, br)

        pos_flat, block_expert, n_used, zero_lo, zero_hi, n_rows = _routing_tables(route, counts, n)
        x_rows = dispatch(xn, pos_flat, zero_lo, zero_hi, n_rows)
        y_rows = experts(x_rows, block_expert, n_used, i, w_gu_b, b_gu4, w_down_b, b_down4)
        h = combine_ple(pos_flat, route, h1, y_rows, p[i].reshape(n, -1), g_ple[i],
                        w_ple_gate[i].astype(bf16), w_ple_proj[i].astype(bf16), g_final,
                        final=(i == depth - 1))
    return h.reshape(batch, seq, d)
```

```python
import functools

import jax
import jax.numpy as jnp
from jax import lax
from jax.experimental import pallas as pl
from jax.experimental.pallas import tpu as pltpu

N_HEADS = 16
HEAD_DIM = 64
N_EXPERTS = 32
TOP_K = 4
SWIGLU_LIMIT = 7.0
SWIGLU_ALPHA = 1.702
RMS_EPS = 1e-6

LANES = 128
HEADS_PER_STEP = LANES // HEAD_DIM
VMEM_LIMIT_BYTES = 48 * 1024 * 1024

ROW_BLOCK = 256
ATTN_BLOCK = 256
TOKEN_BLOCK = 512
GATHER_BLOCK = 256

SB_SKIP_BELOW = -104.0
DMA_LOOP_UNROLL = 8
FOX_KEY_BLOCK = 512

ROUTE_IDX, ROUTE_GATE, ROUTE_RANK = 0, TOP_K, 2 * TOP_K

f32 = jnp.float32
bf16 = jnp.bfloat16


def _params(*semantics):
    return pltpu.CompilerParams(dimension_semantics=semantics, vmem_limit_bytes=VMEM_LIMIT_BYTES)


def _rmsnorm(x, g):
    var = jnp.mean(x * x, axis=-1, keepdims=True)
    return x * lax.rsqrt(var + RMS_EPS) * g


def _store_row_tiles(ref, val):
    for c in range(ref.shape[-2]):
        ref[:, c, :] = val[:, c * LANES:(c + 1) * LANES].astype(ref.dtype)


def _load_row_tiles(ref):
    return jnp.concatenate([ref[:, c, :] for c in range(ref.shape[-2])], axis=1)


def _neg_abs(x):
    bits = lax.bitcast_convert_type(x, jnp.uint32) | jnp.uint32(0x80000000)
    return lax.bitcast_convert_type(bits, f32)


def _split2(x):
    hi = x.astype(bf16)
    lo = (x - hi.astype(f32)).astype(bf16)
    return hi, lo


def _norm_proj_kernel(h_ref, g_ref, w_ref, *rest, with_gate):
    xn = _rmsnorm(h_ref[...], g_ref[...]).astype(bf16)
    if with_gate:
        wf_ref, o_ref, of_ref = rest
        of_ref[...] = jnp.dot(xn, wf_ref[...], preferred_element_type=f32)
    else:
        (o_ref,) = rest
    o_ref[...] = jnp.dot(xn, w_ref[...], preferred_element_type=f32).astype(o_ref.dtype)


def norm_proj(h, g, w, wf=None):
    n, d = h.shape
    m = w.shape[1]
    tm = min(TOKEN_BLOCK, n)
    with_gate = wf is not None
    in_specs = [pl.BlockSpec((tm, d), lambda i: (i, 0)),
                pl.BlockSpec((1, d), lambda i: (0, 0)),
                pl.BlockSpec((d, m), lambda i: (0, 0))]
    out_specs = [pl.BlockSpec((tm, m), lambda i: (i, 0))]
    out_shape = [jax.ShapeDtypeStruct((n, m), bf16)]
    args = [h, g.reshape(1, d), w]
    if with_gate:
        in_specs.append(pl.BlockSpec((d, LANES), lambda i: (0, 0)))
        out_specs.append(pl.BlockSpec((tm, LANES), lambda i: (i, 0)))
        out_shape.append(jax.ShapeDtypeStruct((n, LANES), f32))
        args.append(wf)
    out = pl.pallas_call(
        functools.partial(_norm_proj_kernel, with_gate=with_gate),
        grid=(n // tm,), in_specs=in_specs, out_specs=out_specs, out_shape=out_shape,
        compiler_params=_params("parallel"), name="norm_proj")(*args)
    return out if with_gate else out[0]


def _fox_cumsum_kernel(f_ref, b_ref, tri_ref, c_ref, carry_ref):
    @pl.when(pl.program_id(1) == 0)
    def _():
        carry_ref[...] = jnp.zeros_like(carry_ref)

    x = f_ref[0] + b_ref[...]
    log_f = jnp.minimum(x, 0.0) - jnp.log(1.0 + jnp.exp(-jnp.abs(x)))
    p1 = log_f.astype(bf16)
    r1 = log_f - p1.astype(f32)
    p2 = r1.astype(bf16)
    p3 = (r1 - p2.astype(f32)).astype(bf16)
    tri = tri_ref[...]
    c = (jnp.dot(tri, p1, preferred_element_type=f32)
         + jnp.dot(tri, p2, preferred_element_type=f32)
         + jnp.dot(tri, p3, preferred_element_type=f32)) + carry_ref[...]
    c_ref[0] = c
    carry_ref[...] = c[-1:, :]


def fox_cumsum(f_logit, b_pad, batch, seq):
    ts = min(ATTN_BLOCK, seq)
    tri = (jnp.arange(ts)[None, :] <= jnp.arange(ts)[:, None]).astype(bf16)
    return pl.pallas_call(
        _fox_cumsum_kernel,
        grid=(batch, seq // ts),
        in_specs=[pl.BlockSpec((1, ts, LANES), lambda b, s: (b, s, 0)),
                  pl.BlockSpec((1, LANES), lambda b, s: (0, 0)),
                  pl.BlockSpec((ts, ts), lambda b, s: (0, 0))],
        out_specs=pl.BlockSpec((1, ts, LANES), lambda b, s: (b, s, 0)),
        out_shape=jax.ShapeDtypeStruct((batch, seq, LANES), f32),
        scratch_shapes=[pltpu.VMEM((1, LANES), f32)],
        compiler_params=_params("parallel", "arbitrary"), name="fox_cumsum",
    )(f_logit.reshape(batch, seq, LANES), b_pad, tri)


def _head_mask(hh):
    lane = lax.broadcasted_iota(jnp.int32, (1, LANES), 1)
    return (lane >= hh * HEAD_DIM) & (lane < (hh + 1) * HEAD_DIM)


def _qk(qm, k2):
    return lax.dot_general(qm, k2, (((1,), (1,)), ((), ())), preferred_element_type=f32)


def _sb_attn_kernel(q_ref, k_ref, v_ref, u_ref, o_ref):
    qi = pl.program_id(2)
    tq = q_ref.shape[1]
    tk = tq
    q2 = q_ref[0]
    u2 = u_ref[...]
    masks = [_head_mask(hh) for hh in range(HEADS_PER_STEP)]
    neg_qs = [jnp.where(m, -q2, jnp.zeros_like(q2)) for m in masks]
    heads = range(HEADS_PER_STEP)

    def two_blocks(kb, runs, acc, diag):
        has_next = kb >= 1
        kbs = [kb, jnp.maximum(kb - 1, 0)]
        k2s = [k_ref[0, pl.ds(b * tk, tk), :] for b in kbs]
        v2s = [v_ref[0, pl.ds(b * tk, tk), :] for b in kbs]
        vv = jnp.concatenate([jnp.where(m, v2, jnp.zeros_like(v2)) for v2 in v2s for m in masks], axis=0)
        if diag:
            strict = (lax.broadcasted_iota(jnp.int32, (tq, tk), 1)
                      < lax.broadcasted_iota(jnp.int32, (tq, tk), 0))
        chains = [(blk, hh) for blk in range(2) for hh in heads]
        nzs = {c: _qk(neg_qs[c[1]], k2s[c[0]]) for c in chains}
        suffixes = {}
        for c in chains:
            nz = nzs[c]
            log_not = jnp.minimum(nz, 0.0) - jnp.log(1.0 + jnp.exp(_neg_abs(nz)))
            if diag and c[0] == 0:
                log_not = jnp.where(strict, log_not, 0.0)
            hi, lo = _split2(log_not)
            suffixes[c] = jnp.dot(jnp.concatenate([hi, lo], axis=1), u2, preferred_element_type=f32)
        weights, new_runs = [], []
        for blk in range(2):
            for hh in heads:
                if blk == 0:
                    run = runs[hh]
                else:
                    after = runs[hh] + suffixes[(0, hh)][:, 0:1]
                    run = jnp.where(has_next, after, -jnp.inf)
                    new_runs.append(jnp.where(has_next, after + suffixes[(1, hh)][:, 0:1], after))
                a = jnp.exp((run - nzs[(blk, hh)]) + suffixes[(blk, hh)])
                if diag and blk == 0:
                    a = jnp.where(strict, a, 0.0)
                weights.append(a.astype(bf16))
        acc = acc + jnp.dot(jnp.concatenate(weights, axis=1), vv, preferred_element_type=f32)
        return new_runs, acc

    def alive(runs):
        return (jnp.max(jnp.maximum(runs[0], runs[1])) > SB_SKIP_BELOW).astype(jnp.int32)

    zero_run = jnp.zeros((tq, 1), f32)
    runs, acc = two_blocks(qi, [zero_run, zero_run], jnp.zeros((tq, LANES), f32), True)

    def cond(c):
        return (c[0] >= 0) & (c[1] > 0)

    def body(c):
        kb, _, r0, r1, acc = c
        (r0, r1), acc = two_blocks(kb, [r0, r1], acc, False)
        return kb - 2, alive([r0, r1]), r0, r1, acc

    acc = lax.while_loop(cond, body, (qi - 2, alive(runs), runs[0], runs[1], acc))[4]
    o_ref[0] = acc.astype(o_ref.dtype)


def _fox_attn_kernel(q_ref, k_ref, v_ref, cq_ref, ck_ref, o_ref):
    hp = pl.program_id(1)
    qi = pl.program_id(2)
    tq = q_ref.shape[1]
    tk = min(FOX_KEY_BLOCK, k_ref.shape[1])
    q2 = q_ref[0]
    masks = [_head_mask(hh) for hh in range(HEADS_PER_STEP)]
    qms = [jnp.where(m, q2, jnp.zeros_like(q2)) for m in masks]
    n_full = (qi * tq + 1) // tk
    row = lax.broadcasted_iota(jnp.int32, (tq, tk), 0) + qi * tq
    col = lax.broadcasted_iota(jnp.int32, (tq, tk), 1) + n_full * tk
    causal = col <= row
    lane = lax.broadcasted_iota(jnp.int32, (1, LANES), 1)
    c_blk = cq_ref[0]
    cqs = [jnp.sum(jnp.where(lane == hp * HEADS_PER_STEP + hh, c_blk, 0.0), axis=1, keepdims=True)
           for hh in range(HEADS_PER_STEP)]

    def tile_pair(kb, ms, ls, acc, diag):
        k2 = k_ref[0, pl.ds(kb * tk, tk), :]
        v2 = v_ref[0, pl.ds(kb * tk, tk), :]
        probs, alphas, new_ms, new_ls = [], [], [], []
        for hh in range(HEADS_PER_STEP):
            ck = ck_ref[0, pl.ds(hp * HEADS_PER_STEP + hh, 1), pl.ds(kb * tk, tk)]
            w = _qk(qms[hh], k2) - ck
            if diag:
                w = jnp.where(causal, w, -jnp.inf)
            m_new = jnp.maximum(ms[hh], cqs[hh] + jnp.max(w, axis=1, keepdims=True))
            p = jnp.exp(w - (m_new - cqs[hh]))
            alpha = jnp.exp(ms[hh] - m_new)
            probs.append(p.astype(bf16))
            alphas.append(alpha)
            new_ms.append(m_new)
            new_ls.append(alpha * ls[hh] + jnp.sum(p, axis=1, keepdims=True))
        vv = jnp.concatenate([jnp.where(m, v2, jnp.zeros_like(v2)) for m in masks], axis=0)
        acc = (acc * jnp.where(masks[0], alphas[0], alphas[1])
               + jnp.dot(jnp.concatenate(probs, axis=1), vv, preferred_element_type=f32))
        return new_ms, new_ls, acc

    neg_inf = jnp.full((tq, 1), -jnp.inf, f32)
    zero = jnp.zeros((tq, 1), f32)

    def body(kb, c):
        ms, ls, acc = tile_pair(kb, [c[0], c[1]], [c[2], c[3]], c[4], False)
        return ms[0], ms[1], ls[0], ls[1], acc

    c = lax.fori_loop(0, n_full, body, (neg_inf, neg_inf, zero, zero, jnp.zeros((tq, LANES), f32)))
    ms, ls, acc = tile_pair(n_full, [c[0], c[1]], [c[2], c[3]], c[4], True)
    o_ref[0] = (acc / jnp.where(masks[0], ls[0], ls[1])).astype(o_ref.dtype)


def attention(qkv, batch, seq, c=None):
    d = qkv.shape[1] // 3
    nlb = d // LANES
    tq = min(ATTN_BLOCK, seq)
    qkv3 = qkv.reshape(batch, seq, 3 * d)
    in_specs = [pl.BlockSpec((1, tq, LANES), lambda b, hp, qi: (b, qi, hp)),
                pl.BlockSpec((1, seq, LANES), lambda b, hp, qi: (b, 0, nlb + hp)),
                pl.BlockSpec((1, seq, LANES), lambda b, hp, qi: (b, 0, 2 * nlb + hp))]
    args = [qkv3, qkv3, qkv3]
    if c is None:
        kern = _sb_attn_kernel
        u = (jnp.arange(tq)[:, None] >= jnp.arange(tq)[None, :]).astype(bf16)
        in_specs.append(pl.BlockSpec((2 * tq, tq), lambda b, hp, qi: (0, 0)))
        args.append(jnp.concatenate([u, u], axis=0))
    else:
        kern = _fox_attn_kernel
        ch = jnp.transpose(c[:, :, :N_HEADS], (0, 2, 1))
        in_specs.append(pl.BlockSpec((1, tq, LANES), lambda b, hp, qi: (b, qi, 0)))
        in_specs.append(pl.BlockSpec((1, N_HEADS, seq), lambda b, hp, qi: (b, 0, 0)))
        args += [c, ch]
    out = pl.pallas_call(
        kern, grid=(batch, nlb, seq // tq), in_specs=in_specs,
        out_specs=pl.BlockSpec((1, tq, LANES), lambda b, hp, qi: (b, qi, hp)),
        out_shape=jax.ShapeDtypeStruct((batch, seq, d), bf16),
        compiler_params=_params("parallel", "parallel", "arbitrary"), name="attention")(*args)
    return out.reshape(batch * seq, d)


def _oproj_router_kernel(o_ref, h_ref, wo_ref, g_ref, wrh_ref, wrl_ref, br_ref, tri_ref,
                         h1_ref, xn_ref, route_ref, cnt_ref, carry_ref):
    @pl.when(pl.program_id(0) == 0)
    def _():
        carry_ref[...] = jnp.zeros_like(carry_ref)

    tm = h_ref.shape[0]
    h1 = h_ref[...] + jnp.dot(o_ref[...], wo_ref[...], preferred_element_type=f32)
    h1_ref[...] = h1
    xn = _rmsnorm(h1, g_ref[...])
    _store_row_tiles(xn_ref, xn)

    xh, xl = _split2(xn)
    logits = (jnp.dot(xh, wrh_ref[...], preferred_element_type=f32)
              + jnp.dot(xh, wrl_ref[...], preferred_element_type=f32)
              + jnp.dot(xl, wrh_ref[...], preferred_element_type=f32)) + br_ref[...]
    lane = lax.broadcasted_iota(jnp.int32, (tm, LANES), 1)
    lane_f = lane.astype(f32)
    work = jnp.where(lane < N_EXPERTS, logits, -jnp.inf)

    vals, idxs, hots = [], [], []
    for _ in range(TOP_K):
        mx = jnp.max(work, axis=1, keepdims=True)
        idx = jnp.min(jnp.where(work == mx, lane_f, float(LANES)), axis=1, keepdims=True)
        hot = lane_f == idx
        work = jnp.where(hot, -jnp.inf, work)
        vals.append(mx)
        idxs.append(idx)
        hots.append(hot)

    exps = [jnp.exp(v - vals[0]) for v in vals]
    denom = exps[0] + exps[1] + exps[2] + exps[3]
    gates = [e / denom for e in exps]

    sel = jnp.zeros((tm, LANES), f32)
    for hot in hots:
        sel = sel + hot.astype(f32)
    before = jnp.dot(tri_ref[...], sel.astype(bf16), preferred_element_type=f32) + carry_ref[...]
    ranks = [jnp.sum(jnp.where(hot, before, 0.0), axis=1, keepdims=True) for hot in hots]
    carry = carry_ref[...] + jnp.sum(sel, axis=0, keepdims=True)
    carry_ref[...] = carry
    cnt_ref[...] = jnp.broadcast_to(carry, cnt_ref.shape)

    route = jnp.zeros((tm, LANES), f32)
    for base, cols in ((ROUTE_IDX, idxs), (ROUTE_GATE, gates), (ROUTE_RANK, ranks)):
        for k, colv in enumerate(cols):
            route = jnp.where(lane == base + k, colv, route)
    route_ref[...] = route


def oproj_router(o, h, wo, g, wr_hi, wr_lo, br):
    n, d = h.shape
    tm = min(TOKEN_BLOCK, n)
    tri = (jnp.arange(tm)[None, :] < jnp.arange(tm)[:, None]).astype(bf16)
    row = lambda i: (i, 0)
    fixed = lambda i: (0, 0)
    return pl.pallas_call(
        _oproj_router_kernel,
        grid=(n // tm,),
        in_specs=[pl.BlockSpec((tm, d), row), pl.BlockSpec((tm, d), row),
                  pl.BlockSpec((d, d), fixed), pl.BlockSpec((1, d), fixed),
                  pl.BlockSpec((d, LANES), fixed), pl.BlockSpec((d, LANES), fixed),
                  pl.BlockSpec((1, LANES), fixed), pl.BlockSpec((tm, tm), fixed)],
        out_specs=[pl.BlockSpec((tm, d), row), pl.BlockSpec((tm, d // LANES, LANES), lambda i: (i, 0, 0)),
                   pl.BlockSpec((tm, LANES), row), pl.BlockSpec((8, LANES), fixed)],
        out_shape=[jax.ShapeDtypeStruct((n, d), f32), jax.ShapeDtypeStruct((n, d // LANES, LANES), f32),
                   jax.ShapeDtypeStruct((n, LANES), f32), jax.ShapeDtypeStruct((8, LANES), f32)],
        scratch_shapes=[pltpu.VMEM((1, LANES), f32)],
        compiler_params=_params("arbitrary"), name="oproj_router",
    )(o, h, wo, g.reshape(1, d), wr_hi, wr_lo, br, tri)


def _dispatch_kernel(zlo_ref, zhi_ref, pos_ref, x_ref, rows_ref, zero_ref, sem):
    tm = x_ref.shape[0]

    def row_copy(src, r):
        return pltpu.make_async_copy(src, rows_ref.at[r], sem)

    @pl.when(pl.program_id(0) == 0)
    def _():
        zero_ref[...] = jnp.zeros_like(zero_ref)
        for wait in (False, True):
            def per_expert(e, _, wait=wait):
                def per_row(r, _):
                    cp = row_copy(zero_ref, r)
                    cp.wait() if wait else cp.start()
                    return 0
                return lax.fori_loop(zlo_ref[e], zhi_ref[e], per_row, 0)
            lax.fori_loop(0, N_EXPERTS, per_expert, 0)

    for wait in (False, True):
        def per_token(t, _, wait=wait):
            for k in range(TOP_K):
                cp = row_copy(x_ref.at[t], pos_ref[t * TOP_K + k])
                cp.wait() if wait else cp.start()
            return 0
        lax.fori_loop(0, tm, per_token, 0, unroll=DMA_LOOP_UNROLL)


def dispatch(xn, pos_flat, zero_lo, zero_hi, n_rows):
    n, c, _ = xn.shape
    tm = min(GATHER_BLOCK, n)
    grid_spec = pltpu.PrefetchScalarGridSpec(
        num_scalar_prefetch=2, grid=(n // tm,),
        in_specs=[pl.BlockSpec((tm * TOP_K,), lambda i, *_: (i,), memory_space=pltpu.SMEM),
                  pl.BlockSpec((tm, c, LANES), lambda i, *_: (i, 0, 0))],
        out_specs=pl.BlockSpec(memory_space=pl.ANY),
        scratch_shapes=[pltpu.VMEM((c, LANES), f32), pltpu.SemaphoreType.DMA(())])
    return pl.pallas_call(
        _dispatch_kernel, grid_spec=grid_spec,
        out_shape=jax.ShapeDtypeStruct((n_rows, c, LANES), f32),
        compiler_params=_params("arbitrary"), name="dispatch",
    )(zero_lo, zero_hi, pos_flat, xn)


def _experts_kernel(be_ref, nused_ref, x_ref, wgu_ref, bgu_ref, wd_ref, bd_ref, y_ref):
    rb = pl.program_id(0)
    d_ff = wd_ref.shape[2]

    @pl.when(rb < nused_ref[0])
    def _():
        x = _load_row_tiles(x_ref).astype(bf16)
        hgu = jnp.dot(x, wgu_ref[0, 0], preferred_element_type=f32) + bgu_ref[0, 0]
        glu = jnp.minimum(hgu[:, :d_ff], SWIGLU_LIMIT)
        lin = jnp.clip(hgu[:, d_ff:], -SWIGLU_LIMIT, SWIGLU_LIMIT)
        act = glu * (1.0 / (1.0 + jnp.exp(-SWIGLU_ALPHA * glu))) * (lin + 1.0)
        y = jnp.dot(act.astype(bf16), wd_ref[0, 0], preferred_element_type=f32) + bd_ref[0, 0]
        _store_row_tiles(y_ref, y)

    @pl.when(rb >= nused_ref[0])
    def _():
        y_ref[...] = jnp.zeros_like(y_ref)


def experts(x_rows, block_expert, n_used, layer, w_gu, b_gu, w_down, b_down):
    n_rows, c, _ = x_rows.shape
    d = c * LANES
    f2 = w_gu.shape[3]
    d_ff = f2 // 2
    wmap = lambda rb, be, nu: (layer, be[rb], 0, 0)
    rows = pl.BlockSpec((ROW_BLOCK, c, LANES), lambda rb, be, nu: (rb, 0, 0))
    grid_spec = pltpu.PrefetchScalarGridSpec(
        num_scalar_prefetch=2, grid=(n_rows // ROW_BLOCK,),
        in_specs=[rows, pl.BlockSpec((1, 1, d, f2), wmap), pl.BlockSpec((1, 1, 1, f2), wmap),
                  pl.BlockSpec((1, 1, d_ff, d), wmap), pl.BlockSpec((1, 1, 1, d), wmap)],
        out_specs=rows)
    return pl.pallas_call(
        _experts_kernel, grid_spec=grid_spec,
        out_shape=jax.ShapeDtypeStruct((n_rows, c, LANES), f32),
        compiler_params=_params("arbitrary"), name="experts",
    )(block_expert, n_used, x_rows, w_gu, b_gu, w_down, b_down)


def _combine_ple_kernel(pos_ref, pos_next_ref, route_ref, h_ref, p_ref, g_ref, wg_ref, wp_ref, gf_ref,
                        yrows_ref, o_ref, ybuf_ref, sems, *, final):
    i = pl.program_id(0)
    tm = h_ref.shape[0]
    slot = i % 2

    def gather(idx_ref, s, wait):
        def per_token(t, _):
            for k in range(TOP_K):
                r = 0 if wait else idx_ref[t * TOP_K + k]
                cp = pltpu.make_async_copy(yrows_ref.at[r], ybuf_ref.at[s, k, t], sems.at[s])
                cp.wait() if wait else cp.start()
            return 0
        lax.fori_loop(0, tm, per_token, 0, unroll=DMA_LOOP_UNROLL)

    @pl.when(i == 0)
    def _():
        gather(pos_ref, 0, False)

    @pl.when(i + 1 < pl.num_programs(0))
    def _():
        gather(pos_next_ref, 1 - slot, False)

    proj = jnp.dot(p_ref[...].astype(bf16), wp_ref[...], preferred_element_type=f32)
    gather(None, slot, True)

    route = route_ref[...]
    h2 = h_ref[...]
    for k in range(TOP_K):
        h2 = h2 + route[:, ROUTE_GATE + k:ROUTE_GATE + k + 1] * _load_row_tiles(ybuf_ref.at[slot, k])
    hn = _rmsnorm(h2, g_ref[...]).astype(bf16)
    gate = 1.0 / (1.0 + jnp.exp(-jnp.dot(hn, wg_ref[...], preferred_element_type=f32)))
    h3 = h2 + gate * proj
    if final:
        h3 = _rmsnorm(h3, gf_ref[...])
    o_ref[...] = h3


def combine_ple(pos_flat, route, h1, y_rows, p, g, wg, wp, g_final, final):
    n, d = h1.shape
    pd = p.shape[1]
    tm = min(GATHER_BLOCK, n)
    last = n // tm - 1
    row = lambda i: (i, 0)
    fixed = lambda i: (0, 0)
    return pl.pallas_call(
        functools.partial(_combine_ple_kernel, final=final),
        grid=(n // tm,),
        in_specs=[pl.BlockSpec((tm * TOP_K,), lambda i: (i,), memory_space=pltpu.SMEM),
                  pl.BlockSpec((tm * TOP_K,), lambda i: (jnp.minimum(i + 1, last),), memory_space=pltpu.SMEM),
                  pl.BlockSpec((tm, LANES), row), pl.BlockSpec((tm, d), row),
                  pl.BlockSpec((tm, pd), row), pl.BlockSpec((1, d), fixed),
                  pl.BlockSpec((d, d), fixed), pl.BlockSpec((pd, d), fixed),
                  pl.BlockSpec((1, d), fixed), pl.BlockSpec(memory_space=pl.ANY)],
        out_specs=pl.BlockSpec((tm, d), row),
        out_shape=jax.ShapeDtypeStruct((n, d), f32),
        scratch_shapes=[pltpu.VMEM((2, TOP_K, tm, d // LANES, LANES), f32), pltpu.SemaphoreType.DMA((2,))],
        compiler_params=_params("arbitrary"), name="combine_ple",
    )(pos_flat, pos_flat, route, h1, p, g.reshape(1, d), wg, wp, g_final.reshape(1, d), y_rows)


def _routing_tables(route, counts_f, n):
    counts = counts_f[0, :N_EXPERTS].astype(jnp.int32)
    padded = (counts + ROW_BLOCK - 1) // ROW_BLOCK * ROW_BLOCK
    pad_end = jnp.cumsum(padded)
    pad_start = pad_end - padded
    n_blocks = -(-(n * TOP_K + N_EXPERTS * (ROW_BLOCK - 1)) // ROW_BLOCK)
    n_rows = n_blocks * ROW_BLOCK
    idx = route[:, ROUTE_IDX:ROUTE_IDX + TOP_K].astype(jnp.int32)
    rank = route[:, ROUTE_RANK:ROUTE_RANK + TOP_K].astype(jnp.int32)
    pos_flat = (pad_start[idx] + rank).reshape(-1)
    block_start = jnp.arange(n_blocks, dtype=jnp.int32) * ROW_BLOCK
    block_expert = jnp.minimum(jnp.sum(block_start[:, None] >= pad_end[None, :], axis=1),
                               N_EXPERTS - 1).astype(jnp.int32)
    n_used = (pad_end[-1:] // ROW_BLOCK).astype(jnp.int32)
    zero_lo = (pad_start + counts).astype(jnp.int32)
    zero_hi = jnp.concatenate([pad_start[1:], jnp.array([n_rows], jnp.int32)]).astype(jnp.int32)
    return pos_flat, block_expert, n_used, zero_lo, zero_hi, n_rows


def kernel(x, p, g_mix, w_qkv, w_fgate, b_fgate, w_o, g_moe, w_router, b_router, w_gu, b_gu,
           w_down, b_down, g_ple, w_ple_gate, w_ple_proj, g_final):
    batch, seq, d = x.shape
    depth = w_qkv.shape[0]
    n = batch * seq
    h = x.reshape(n, d)
    q_scale = jnp.concatenate([jnp.full((d,), HEAD_DIM ** -0.5, f32), jnp.ones((2 * d,), f32)])
    n_exp = w_gu.shape[1]
    w_gu_b, w_down_b = w_gu.astype(bf16), w_down.astype(bf16)
    b_gu4 = b_gu.reshape(depth, n_exp, 1, -1)
    b_down4 = b_down.reshape(depth, n_exp, 1, -1)
    for i in range(depth):
        wqkv = (w_qkv[i] * q_scale).astype(bf16)
        if i % 2 == 0:
            qkv = norm_proj(h, g_mix[i], wqkv)
            o = attention(qkv, batch, seq)
        else:
            j = i // 2
            wf = jnp.pad(w_fgate[j], ((0, 0), (0, LANES - N_HEADS))).astype(bf16)
            bf = jnp.pad(b_fgate[j], (0, LANES - N_HEADS)).reshape(1, LANES)
            qkv, f_logit = norm_proj(h, g_mix[i], wqkv, wf)
            c = fox_cumsum(f_logit, bf, batch, seq)
            o = attention(qkv, batch, seq, c)

        wr = jnp.pad(w_router[i], ((0, 0), (0, LANES - N_EXPERTS)))
        wr_hi = wr.astype(bf16)
        wr_lo = (wr - wr_hi.astype(f32)).astype(bf16)
        br = jnp.pad(b_router[i], (0, LANES - N_EXPERTS)).reshape(1, LANES)
        h1, xn, route, counts = oproj_router(o, h, w_o[i].astype(bf16), g_moe[i], wr_hi, wr_lo, br)

        pos_flat, block_expert, n_used, zero_lo, zero_hi, n_rows = _routing_tables(route, counts, n)
        x_rows = dispatch(xn, pos_flat, zero_lo, zero_hi, n_rows)
        y_rows = experts(x_rows, block_expert, n_used, i, w_gu_b, b_gu4, w_down_b, b_down4)
        h = combine_ple(pos_flat, route, h1, y_rows, p[i].reshape(n, -1), g_ple[i],
                        w_ple_gate[i].astype(bf16), w_ple_proj[i].astype(bf16), g_final,
                        final=(i == depth - 1))
    return h.reshape(batch, seq, d)
```

```python
import functools

import jax
import jax.numpy as jnp
from jax import lax
from jax.experimental import pallas as pl
from jax.experimental.pallas import tpu as pltpu

N_HEADS = 16
HEAD_DIM = 64
N_EXPERTS = 32
TOP_K = 4
SWIGLU_LIMIT = 7.0
SWIGLU_ALPHA = 1.702
RMS_EPS = 1e-6

LANES = 128
HEADS_PER_STEP = LANES // HEAD_DIM
VMEM_LIMIT_BYTES = 48 * 1024 * 1024

ROW_BLOCK = 256
ATTN_BLOCK = 256
TOKEN_BLOCK = 512
GATHER_BLOCK = 256

SB_SKIP_BELOW = -104.0
DMA_LOOP_UNROLL = 8
FOX_LANE_GROUPS = 2
FOX_KEY_BLOCK = 512

ROUTE_IDX, ROUTE_GATE, ROUTE_RANK = 0, TOP_K, 2 * TOP_K

f32 = jnp.float32
bf16 = jnp.bfloat16


def _params(*semantics):
    return pltpu.CompilerParams(dimension_semantics=semantics, vmem_limit_bytes=VMEM_LIMIT_BYTES)


def _rmsnorm(x, g):
    var = jnp.mean(x * x, axis=-1, keepdims=True)
    return x * lax.rsqrt(var + RMS_EPS) * g


def _neg_abs(x):
    bits = lax.bitcast_convert_type(x, jnp.uint32) | jnp.uint32(0x80000000)
    return lax.bitcast_convert_type(bits, f32)


def _split2(x):
    hi = x.astype(bf16)
    lo = (x - hi.astype(f32)).astype(bf16)
    return hi, lo


def _norm_proj_kernel(h_ref, g_ref, w_ref, *rest, with_gate):
    xn = _rmsnorm(h_ref[...], g_ref[...]).astype(bf16)
    if with_gate:
        wf_ref, o_ref, of_ref = rest
        of_ref[...] = jnp.dot(xn, wf_ref[...], preferred_element_type=f32)
    else:
        (o_ref,) = rest
    o_ref[...] = jnp.dot(xn, w_ref[...], preferred_element_type=f32).astype(o_ref.dtype)


def norm_proj(h, g, w, wf=None):
    n, d = h.shape
    m = w.shape[1]
    tm = min(TOKEN_BLOCK, n)
    with_gate = wf is not None
    in_specs = [pl.BlockSpec((tm, d), lambda i: (i, 0)),
                pl.BlockSpec((1, d), lambda i: (0, 0)),
                pl.BlockSpec((d, m), lambda i: (0, 0))]
    out_specs = [pl.BlockSpec((tm, m), lambda i: (i, 0))]
    out_shape = [jax.ShapeDtypeStruct((n, m), bf16)]
    args = [h, g.reshape(1, d), w]
    if with_gate:
        in_specs.append(pl.BlockSpec((d, LANES), lambda i: (0, 0)))
        out_specs.append(pl.BlockSpec((tm, LANES), lambda i: (i, 0)))
        out_shape.append(jax.ShapeDtypeStruct((n, LANES), f32))
        args.append(wf)
    out = pl.pallas_call(
        functools.partial(_norm_proj_kernel, with_gate=with_gate),
        grid=(n // tm,), in_specs=in_specs, out_specs=out_specs, out_shape=out_shape,
        compiler_params=_params("parallel"), name="norm_proj")(*args)
    return out if with_gate else out[0]


def _fox_cumsum_kernel(f_ref, b_ref, tri_ref, c_ref, carry_ref):
    @pl.when(pl.program_id(1) == 0)
    def _():
        carry_ref[...] = jnp.zeros_like(carry_ref)

    x = f_ref[0] + b_ref[...]
    log_f = jnp.minimum(x, 0.0) - jnp.log(1.0 + jnp.exp(-jnp.abs(x)))
    p1 = log_f.astype(bf16)
    r1 = log_f - p1.astype(f32)
    p2 = r1.astype(bf16)
    p3 = (r1 - p2.astype(f32)).astype(bf16)
    tri = tri_ref[...]
    c = (jnp.dot(tri, p1, preferred_element_type=f32)
         + jnp.dot(tri, p2, preferred_element_type=f32)
         + jnp.dot(tri, p3, preferred_element_type=f32)) + carry_ref[...]
    c_ref[0] = c
    carry_ref[...] = c[-1:, :]


def fox_cumsum(f_logit, b_pad, batch, seq):
    ts = min(ATTN_BLOCK, seq)
    tri = (jnp.arange(ts)[None, :] <= jnp.arange(ts)[:, None]).astype(bf16)
    return pl.pallas_call(
        _fox_cumsum_kernel,
        grid=(batch, seq // ts),
        in_specs=[pl.BlockSpec((1, ts, LANES), lambda b, s: (b, s, 0)),
                  pl.BlockSpec((1, LANES), lambda b, s: (0, 0)),
                  pl.BlockSpec((ts, ts), lambda b, s: (0, 0))],
        out_specs=pl.BlockSpec((1, ts, LANES), lambda b, s: (b, s, 0)),
        out_shape=jax.ShapeDtypeStruct((batch, seq, LANES), f32),
        scratch_shapes=[pltpu.VMEM((1, LANES), f32)],
        compiler_params=_params("parallel", "arbitrary"), name="fox_cumsum",
    )(f_logit.reshape(batch, seq, LANES), b_pad, tri)


def _head_mask(hh):
    lane = lax.broadcasted_iota(jnp.int32, (1, LANES), 1)
    return (lane >= hh * HEAD_DIM) & (lane < (hh + 1) * HEAD_DIM)


def _qk(qm, k2):
    return lax.dot_general(qm, k2, (((1,), (1,)), ((), ())), preferred_element_type=f32)


def _sb_attn_kernel(q_ref, k_ref, v_ref, u_ref, o_ref):
    qi = pl.program_id(2)
    tq = q_ref.shape[1]
    tk = tq
    q2 = q_ref[0]
    u2 = u_ref[...]
    masks = [_head_mask(hh) for hh in range(HEADS_PER_STEP)]
    neg_qs = [jnp.where(m, -q2, jnp.zeros_like(q2)) for m in masks]
    heads = range(HEADS_PER_STEP)

    def two_blocks(kb, runs, acc, diag):
        has_next = kb >= 1
        kbs = [kb, jnp.maximum(kb - 1, 0)]
        k2s = [k_ref[0, pl.ds(b * tk, tk), :] for b in kbs]
        v2s = [v_ref[0, pl.ds(b * tk, tk), :] for b in kbs]
        vv = jnp.concatenate([jnp.where(m, v2, jnp.zeros_like(v2)) for v2 in v2s for m in masks], axis=0)
        if diag:
            strict = (lax.broadcasted_iota(jnp.int32, (tq, tk), 1)
                      < lax.broadcasted_iota(jnp.int32, (tq, tk), 0))
        chains = [(blk, hh) for blk in range(2) for hh in heads]
        nzs = {c: _qk(neg_qs[c[1]], k2s[c[0]]) for c in chains}
        suffixes = {}
        for c in chains:
            nz = nzs[c]
            log_not = jnp.minimum(nz, 0.0) - jnp.log(1.0 + jnp.exp(_neg_abs(nz)))
            if diag and c[0] == 0:
                log_not = jnp.where(strict, log_not, 0.0)
            hi, lo = _split2(log_not)
            suffixes[c] = jnp.dot(jnp.concatenate([hi, lo], axis=1), u2, preferred_element_type=f32)
        weights, new_runs = [], []
        for blk in range(2):
            for hh in heads:
                if blk == 0:
                    run = runs[hh]
                else:
                    after = runs[hh] + suffixes[(0, hh)][:, 0:1]
                    run = jnp.where(has_next, after, -jnp.inf)
                    new_runs.append(jnp.where(has_next, after + suffixes[(1, hh)][:, 0:1], after))
                a = jnp.exp((run - nzs[(blk, hh)]) + suffixes[(blk, hh)])
                if diag and blk == 0:
                    a = jnp.where(strict, a, 0.0)
                weights.append(a.astype(bf16))
        acc = acc + jnp.dot(jnp.concatenate(weights, axis=1), vv, preferred_element_type=f32)
        return new_runs, acc

    def alive(runs):
        return (jnp.max(jnp.maximum(runs[0], runs[1])) > SB_SKIP_BELOW).astype(jnp.int32)

    zero_run = jnp.zeros((tq, 1), f32)
    runs, acc = two_blocks(qi, [zero_run, zero_run], jnp.zeros((tq, LANES), f32), True)

    def cond(c):
        return (c[0] >= 0) & (c[1] > 0)

    def body(c):
        kb, _, r0, r1, acc = c
        (r0, r1), acc = two_blocks(kb, [r0, r1], acc, False)
        return kb - 2, alive([r0, r1]), r0, r1, acc

    acc = lax.while_loop(cond, body, (qi - 2, alive(runs), runs[0], runs[1], acc))[4]
    o_ref[0] = acc.astype(o_ref.dtype)


def _fox_attn_kernel(q_ref, k_ref, v_ref, cq_ref, ck_ref, o_ref):
    hp = pl.program_id(1)
    qi = pl.program_id(2)
    tq = q_ref.shape[1]
    tk = min(FOX_KEY_BLOCK, k_ref.shape[1])
    groups = q_ref.shape[2] // LANES
    masks = [_head_mask(hh) for hh in range(HEADS_PER_STEP)]
    q0 = qi * tq
    n_full = q0 // tk
    lane = lax.broadcasted_iota(jnp.int32, (1, LANES), 1)
    c_blk = cq_ref[0]
    qms, cq_ps = [], []
    for g in range(groups):
        q2 = q_ref[0, :, g * LANES:(g + 1) * LANES]
        qms.append([jnp.where(m, q2, jnp.zeros_like(q2)) for m in masks])
        head0 = (hp * groups + g) * HEADS_PER_STEP
        cqs = [jnp.sum(jnp.where(lane == head0 + hh, c_blk, 0.0), axis=1, keepdims=True)
               for hh in range(HEADS_PER_STEP)]
        cq_ps.append(jnp.where(masks[0], cqs[0], cqs[1]))

    def step(k_start, width, masked, carry):
        k_start = pl.multiple_of(k_start, LANES)
        if masked:
            visible = (lax.broadcasted_iota(jnp.int32, (tq, width), 1) + k_start
                       <= lax.broadcasted_iota(jnp.int32, (tq, width), 0) + q0)
        ws = []
        for g in range(groups):
            k2 = k_ref[0, pl.ds(k_start, width), g * LANES:(g + 1) * LANES]
            for hh in range(HEADS_PER_STEP):
                head = (hp * groups + g) * HEADS_PER_STEP + hh
                ck = ck_ref[0, pl.ds(head, 1), pl.ds(k_start, width)]
                w = _qk(qms[g][hh], k2) - ck
                ws.append(jnp.where(visible, w, -jnp.inf) if masked else w)
        out = []
        for g in range(groups):
            m_p, l_p, acc = carry[g]
            w0, w1 = ws[g * HEADS_PER_STEP], ws[g * HEADS_PER_STEP + 1]
            mx_p = jnp.where(masks[0], jnp.max(w0, axis=1, keepdims=True), jnp.max(w1, axis=1, keepdims=True))
            m_new = jnp.maximum(m_p, cq_ps[g] + mx_p)
            shift = m_new - cq_ps[g]
            alpha = jnp.exp(m_p - m_new)
            p0 = jnp.exp(w0 - shift[:, 0:1])
            p1 = jnp.exp(w1 - shift[:, HEAD_DIM:HEAD_DIM + 1])
            sum_p = jnp.where(masks[0], jnp.sum(p0, axis=1, keepdims=True), jnp.sum(p1, axis=1, keepdims=True))
            v2 = v_ref[0, pl.ds(k_start, width), g * LANES:(g + 1) * LANES]
            vv = jnp.concatenate([jnp.where(m, v2, jnp.zeros_like(v2)) for m in masks], axis=0)
            acc = acc * alpha + jnp.dot(jnp.concatenate([p0.astype(bf16), p1.astype(bf16)], axis=1), vv,
                                        preferred_element_type=f32)
            out.append((m_new, alpha * l_p + sum_p, acc))
        return tuple(out)

    init = tuple((jnp.full((tq, LANES), -jnp.inf, f32), jnp.zeros((tq, LANES), f32),
                  jnp.zeros((tq, LANES), f32)) for _ in range(groups))
    c = lax.fori_loop(0, n_full, lambda kb, c: step(kb * tk, tk, False, c), init)
    if tk == tq:
        c = step(q0, tq, True, c)
    else:
        c = lax.cond(q0 == n_full * tk,
                     lambda c: step(q0, tq, True, c),
                     lambda c: step(n_full * tk, tk, True, c), c)
    o_ref[0] = jnp.concatenate([acc / l_p for _, l_p, acc in c], axis=1).astype(o_ref.dtype)


def attention(qkv, batch, seq, c=None):
    d = qkv.shape[1] // 3
    width = LANES * (1 if c is None else FOX_LANE_GROUPS)
    nlb = d // width
    tq = min(ATTN_BLOCK, seq)
    qkv3 = qkv.reshape(batch, seq, 3 * d)
    in_specs = [pl.BlockSpec((1, tq, width), lambda b, hp, qi: (b, qi, hp)),
                pl.BlockSpec((1, seq, width), lambda b, hp, qi: (b, 0, nlb + hp)),
                pl.BlockSpec((1, seq, width), lambda b, hp, qi: (b, 0, 2 * nlb + hp))]
    args = [qkv3, qkv3, qkv3]
    if c is None:
        kern = _sb_attn_kernel
        u = (jnp.arange(tq)[:, None] >= jnp.arange(tq)[None, :]).astype(bf16)
        in_specs.append(pl.BlockSpec((2 * tq, tq), lambda b, hp, qi: (0, 0)))
        args.append(jnp.concatenate([u, u], axis=0))
    else:
        kern = _fox_attn_kernel
        tk = min(FOX_KEY_BLOCK, seq)
        assert tk in (tq, 2 * tq) and seq % tk == 0
        ch = jnp.transpose(c[:, :, :N_HEADS], (0, 2, 1))
        in_specs.append(pl.BlockSpec((1, tq, LANES), lambda b, hp, qi: (b, qi, 0)))
        in_specs.append(pl.BlockSpec((1, N_HEADS, seq), lambda b, hp, qi: (b, 0, 0)))
        args += [c, ch]
    out = pl.pallas_call(
        kern, grid=(batch, nlb, seq // tq), in_specs=in_specs,
        out_specs=pl.BlockSpec((1, tq, width), lambda b, hp, qi: (b, qi, hp)),
        out_shape=jax.ShapeDtypeStruct((batch, seq, d), bf16),
        compiler_params=_params("parallel", "parallel", "arbitrary"), name="attention")(*args)
    return out.reshape(batch * seq, d)


def _oproj_router_kernel(o_ref, h_ref, wo_ref, g_ref, wrh_ref, wrl_ref, br_ref, tri_ref,
                         h1_ref, xn_ref, route_ref, cnt_ref, carry_ref):
    @pl.when(pl.program_id(0) == 0)
    def _():
        carry_ref[...] = jnp.zeros_like(carry_ref)

    tm = h_ref.shape[0]
    h1 = h_ref[...] + jnp.dot(o_ref[...], wo_ref[...], preferred_element_type=f32)
    h1_ref[...] = h1
    xn = _rmsnorm(h1, g_ref[...])
    xn_ref[...] = xn

    xh, xl = _split2(xn)
    logits = (jnp.dot(xh, wrh_ref[...], preferred_element_type=f32)
              + jnp.dot(xh, wrl_ref[...], preferred_element_type=f32)
              + jnp.dot(xl, wrh_ref[...], preferred_element_type=f32)) + br_ref[...]
    lane = lax.broadcasted_iota(jnp.int32, (tm, LANES), 1)
    lane_f = lane.astype(f32)
    work = jnp.where(lane < N_EXPERTS, logits, -jnp.inf)

    vals, idxs, hots = [], [], []
    for _ in range(TOP_K):
        mx = jnp.max(work, axis=1, keepdims=True)
        idx = jnp.min(jnp.where(work == mx, lane_f, float(LANES)), axis=1, keepdims=True)
        hot = lane_f == idx
        work = jnp.where(hot, -jnp.inf, work)
        vals.append(mx)
        idxs.append(idx)
        hots.append(hot)

    exps = [jnp.exp(v - vals[0]) for v in vals]
    denom = exps[0] + exps[1] + exps[2] + exps[3]
    gates = [e / denom for e in exps]

    sel = jnp.zeros((tm, LANES), f32)
    for hot in hots:
        sel = sel + hot.astype(f32)
    before = jnp.dot(tri_ref[...], sel.astype(bf16), preferred_element_type=f32) + carry_ref[...]
    ranks = [jnp.sum(jnp.where(hot, before, 0.0), axis=1, keepdims=True) for hot in hots]
    carry = carry_ref[...] + jnp.sum(sel, axis=0, keepdims=True)
    carry_ref[...] = carry
    cnt_ref[...] = jnp.broadcast_to(carry, cnt_ref.shape)

    route = jnp.zeros((tm, LANES), f32)
    for base, cols in ((ROUTE_IDX, idxs), (ROUTE_GATE, gates), (ROUTE_RANK, ranks)):
        for k, colv in enumerate(cols):
            route = jnp.where(lane == base + k, colv, route)
    route_ref[...] = route


def oproj_router(o, h, wo, g, wr_hi, wr_lo, br):
    n, d = h.shape
    tm = min(TOKEN_BLOCK, n)
    tri = (jnp.arange(tm)[None, :] < jnp.arange(tm)[:, None]).astype(bf16)
    row = lambda i: (i, 0)
    fixed = lambda i: (0, 0)
    return pl.pallas_call(
        _oproj_router_kernel,
        grid=(n // tm,),
        in_specs=[pl.BlockSpec((tm, d), row), pl.BlockSpec((tm, d), row),
                  pl.BlockSpec((d, d), fixed), pl.BlockSpec((1, d), fixed),
                  pl.BlockSpec((d, LANES), fixed), pl.BlockSpec((d, LANES), fixed),
                  pl.BlockSpec((1, LANES), fixed), pl.BlockSpec((tm, tm), fixed)],
        out_specs=[pl.BlockSpec((tm, d), row), pl.BlockSpec((tm, d), row),
                   pl.BlockSpec((tm, LANES), row), pl.BlockSpec((8, LANES), fixed)],
        out_shape=[jax.ShapeDtypeStruct((n, d), f32), jax.ShapeDtypeStruct((n, d), f32),
                   jax.ShapeDtypeStruct((n, LANES), f32), jax.ShapeDtypeStruct((8, LANES), f32)],
        scratch_shapes=[pltpu.VMEM((1, LANES), f32)],
        compiler_params=_params("arbitrary"), name="oproj_router",
    )(o, h, wo, g.reshape(1, d), wr_hi, wr_lo, br, tri)


def _dispatch_kernel(zlo_ref, zhi_ref, pos_ref, x_ref, rows_ref, zero_ref, sem):
    tm = x_ref.shape[0]

    def row_copy(src, r):
        return pltpu.make_async_copy(src, rows_ref.at[pl.ds(r, 1)], sem)

    @pl.when(pl.program_id(0) == 0)
    def _():
        zero_ref[...] = jnp.zeros_like(zero_ref)
        for wait in (False, True):
            def per_expert(e, _, wait=wait):
                def per_row(r, _):
                    cp = row_copy(zero_ref, r)
                    cp.wait() if wait else cp.start()
                    return 0
                return lax.fori_loop(zlo_ref[e], zhi_ref[e], per_row, 0)
            lax.fori_loop(0, N_EXPERTS, per_expert, 0)

    for wait in (False, True):
        def per_token(t, _, wait=wait):
            for k in range(TOP_K):
                cp = row_copy(x_ref.at[pl.ds(t, 1)], pos_ref[t * TOP_K + k])
                cp.wait() if wait else cp.start(priority=k % 2)
            return 0
        lax.fori_loop(0, tm, per_token, 0, unroll=DMA_LOOP_UNROLL)


def dispatch(xn, pos_flat, zero_lo, zero_hi, n_rows):
    n, d = xn.shape
    tm = min(GATHER_BLOCK, n)
    grid_spec = pltpu.PrefetchScalarGridSpec(
        num_scalar_prefetch=2, grid=(n // tm,),
        in_specs=[pl.BlockSpec((tm * TOP_K,), lambda i, *_: (i,), memory_space=pltpu.SMEM),
                  pl.BlockSpec((tm, d), lambda i, *_: (i, 0))],
        out_specs=pl.BlockSpec(memory_space=pl.ANY),
        scratch_shapes=[pltpu.VMEM((1, d), f32), pltpu.SemaphoreType.DMA(())])
    return pl.pallas_call(
        _dispatch_kernel, grid_spec=grid_spec,
        out_shape=jax.ShapeDtypeStruct((n_rows, d), f32),
        compiler_params=_params("arbitrary"), name="dispatch",
    )(zero_lo, zero_hi, pos_flat, xn)


def _experts_kernel(be_ref, nused_ref, x_ref, wgu_ref, bgu_ref, wd_ref, bd_ref, y_ref):
    rb = pl.program_id(0)
    d_ff = wd_ref.shape[2]

    @pl.when(rb < nused_ref[0])
    def _():
        x = x_ref[...].astype(bf16)
        hgu = jnp.dot(x, wgu_ref[0, 0], preferred_element_type=f32) + bgu_ref[0, 0]
        glu = jnp.minimum(hgu[:, :d_ff], SWIGLU_LIMIT)
        lin = jnp.clip(hgu[:, d_ff:], -SWIGLU_LIMIT, SWIGLU_LIMIT)
        act = glu * (1.0 / (1.0 + jnp.exp(-SWIGLU_ALPHA * glu))) * (lin + 1.0)
        y_ref[...] = jnp.dot(act.astype(bf16), wd_ref[0, 0], preferred_element_type=f32) + bd_ref[0, 0]

    @pl.when(rb >= nused_ref[0])
    def _():
        y_ref[...] = jnp.zeros_like(y_ref)


def experts(x_rows, block_expert, n_used, layer, w_gu, b_gu, w_down, b_down):
    n_rows, d = x_rows.shape
    f2 = w_gu.shape[3]
    d_ff = f2 // 2
    wmap = lambda rb, be, nu: (layer, be[rb], 0, 0)
    rows = pl.BlockSpec((ROW_BLOCK, d), lambda rb, be, nu: (rb, 0))
    grid_spec = pltpu.PrefetchScalarGridSpec(
        num_scalar_prefetch=2, grid=(n_rows // ROW_BLOCK,),
        in_specs=[rows, pl.BlockSpec((1, 1, d, f2), wmap), pl.BlockSpec((1, 1, 1, f2), wmap),
                  pl.BlockSpec((1, 1, d_ff, d), wmap), pl.BlockSpec((1, 1, 1, d), wmap)],
        out_specs=rows)
    return pl.pallas_call(
        _experts_kernel, grid_spec=grid_spec,
        out_shape=jax.ShapeDtypeStruct((n_rows, d), f32),
        compiler_params=_params("arbitrary"), name="experts",
    )(block_expert, n_used, x_rows, w_gu, b_gu, w_down, b_down)


def _combine_ple_kernel(pos_ref, pos_next_ref, route_ref, h_ref, p_ref, g_ref, wg_ref, wp_ref, gf_ref,
                        yrows_ref, o_ref, ybuf_ref, sems, *, final):
    i = pl.program_id(0)
    tm = h_ref.shape[0]
    slot = i % 2

    def gather(idx_ref, s, wait):
        def per_token(t, _):
            for k in range(TOP_K):
                r = 0 if wait else idx_ref[t * TOP_K + k]
                cp = pltpu.make_async_copy(yrows_ref.at[pl.ds(r, 1)], ybuf_ref.at[s, k, pl.ds(t, 1)],
                                           sems.at[s])
                cp.wait() if wait else cp.start(priority=k % 2)
            return 0
        lax.fori_loop(0, tm, per_token, 0, unroll=DMA_LOOP_UNROLL)

    @pl.when(i == 0)
    def _():
        gather(pos_ref, 0, False)

    @pl.when(i + 1 < pl.num_programs(0))
    def _():
        gather(pos_next_ref, 1 - slot, False)

    proj = jnp.dot(p_ref[...].astype(bf16), wp_ref[...], preferred_element_type=f32)
    gather(None, slot, True)

    route = route_ref[...]
    h2 = h_ref[...]
    for k in range(TOP_K):
        h2 = h2 + route[:, ROUTE_GATE + k:ROUTE_GATE + k + 1] * ybuf_ref[slot, k]
    hn = _rmsnorm(h2, g_ref[...]).astype(bf16)
    gate = 1.0 / (1.0 + jnp.exp(-jnp.dot(hn, wg_ref[...], preferred_element_type=f32)))
    h3 = h2 + gate * proj
    if final:
        h3 = _rmsnorm(h3, gf_ref[...])
    o_ref[...] = h3


def combine_ple(pos_flat, route, h1, y_rows, p, g, wg, wp, g_final, final):
    n, d = h1.shape
    pd = p.shape[1]
    tm = min(GATHER_BLOCK, n)
    last = n // tm - 1
    row = lambda i: (i, 0)
    fixed = lambda i: (0, 0)
    return pl.pallas_call(
        functools.partial(_combine_ple_kernel, final=final),
        grid=(n // tm,),
        in_specs=[pl.BlockSpec((tm * TOP_K,), lambda i: (i,), memory_space=pltpu.SMEM),
                  pl.BlockSpec((tm * TOP_K,), lambda i: (jnp.minimum(i + 1, last),), memory_space=pltpu.SMEM),
                  pl.BlockSpec((tm, LANES), row), pl.BlockSpec((tm, d), row),
                  pl.BlockSpec((tm, pd), row), pl.BlockSpec((1, d), fixed),
                  pl.BlockSpec((d, d), fixed), pl.BlockSpec((pd, d), fixed),
                  pl.BlockSpec((1, d), fixed), pl.BlockSpec(memory_space=pl.ANY)],
        out_specs=pl.BlockSpec((tm, d), row),
        out_shape=jax.ShapeDtypeStruct((n, d), f32),
        scratch_shapes=[pltpu.VMEM((2, TOP_K, tm, d), f32), pltpu.SemaphoreType.DMA((2,))],
        compiler_params=_params("arbitrary"), name="combine_ple",
    )(pos_flat, pos_flat, route, h1, p, g.reshape(1, d), wg, wp, g_final.reshape(1, d), y_rows)


def _routing_tables(route, counts_f, n):
    counts = counts_f[0, :N_EXPERTS].astype(jnp.int32)
    padded = (counts + ROW_BLOCK - 1) // ROW_BLOCK * ROW_BLOCK
    pad_end = jnp.cumsum(padded)
    pad_start = pad_end - padded
    n_blocks = -(-(n * TOP_K + N_EXPERTS * (ROW_BLOCK - 1)) // ROW_BLOCK)
    n_rows = n_blocks * ROW_BLOCK
    idx = route[:, ROUTE_IDX:ROUTE_IDX + TOP_K].astype(jnp.int32)
    rank = route[:, ROUTE_RANK:ROUTE_RANK + TOP_K].astype(jnp.int32)
    pos_flat = (pad_start[idx] + rank).reshape(-1)
    block_start = jnp.arange(n_blocks, dtype=jnp.int32) * ROW_BLOCK
    block_expert = jnp.minimum(jnp.sum(block_start[:, None] >= pad_end[None, :], axis=1),
                               N_EXPERTS - 1).astype(jnp.int32)
    n_used = (pad_end[-1:] // ROW_BLOCK).astype(jnp.int32)
    zero_lo = (pad_start + counts).astype(jnp.int32)
    zero_hi = jnp.concatenate([pad_start[1:], jnp.array([n_rows], jnp.int32)]).astype(jnp.int32)
    return pos_flat, block_expert, n_used, zero_lo, zero_hi, n_rows


def kernel(x, p, g_mix, w_qkv, w_fgate, b_fgate, w_o, g_moe, w_router, b_router, w_gu, b_gu,
           w_down, b_down, g_ple, w_ple_gate, w_ple_proj, g_final):
    batch, seq, d = x.shape
    depth = w_qkv.shape[0]
    n = batch * seq
    h = x.reshape(n, d)
    q_scale = jnp.concatenate([jnp.full((d,), HEAD_DIM ** -0.5, f32), jnp.ones((2 * d,), f32)])
    n_exp = w_gu.shape[1]
    w_gu_b, w_down_b = w_gu.astype(bf16), w_down.astype(bf16)
    b_gu4 = b_gu.reshape(depth, n_exp, 1, -1)
    b_down4 = b_down.reshape(depth, n_exp, 1, -1)
    for i in range(depth):
        wqkv = (w_qkv[i] * q_scale).astype(bf16)
        if i % 2 == 0:
            qkv = norm_proj(h, g_mix[i], wqkv)
            o = attention(qkv, batch, seq)
        else:
            j = i // 2
            wf = jnp.pad(w_fgate[j], ((0, 0), (0, LANES - N_HEADS))).astype(bf16)
            bf = jnp.pad(b_fgate[j], (0, LANES - N_HEADS)).reshape(1, LANES)
            qkv, f_logit = norm_proj(h, g_mix[i], wqkv, wf)
            c = fox_cumsum(f_logit, bf, batch, seq)
            o = attention(qkv, batch, seq, c)

        wr = jnp.pad(w_router[i], ((0, 0), (0, LANES - N_EXPERTS)))
        wr_hi = wr.astype(bf16)
        wr_lo = (wr - wr_hi.astype(f32)).astype(bf16)
        br = jnp.pad(b_router[i], (0, LANES - N_EXPERTS)).reshape(1, LANES)
        h1, xn, route, counts = oproj_router(o, h, w_o[i].astype(bf16), g_moe[i], wr_hi, wr_lo, br)

        pos_flat, block_expert, n_used, zero_lo, zero_hi, n_rows = _routing_tables(route, counts, n)
        x_rows = dispatch(xn, pos_flat, zero_lo, zero_hi, n_rows)
        y_rows = experts(x_rows, block_expert, n_used, i, w_gu_b, b_gu4, w_down_b, b_down4)
        h = combine_ple(pos_flat, route, h1, y_rows, p[i].reshape(n, -1), g_ple[i],
                        w_ple_gate[i].astype(bf16), w_ple_proj[i].astype(bf16), g_final,
                        final=(i == depth - 1))
    return h.reshape(batch, seq, d)
```

```python
import functools

import jax
import jax.numpy as jnp
from jax import lax
from jax.experimental import pallas as pl
from jax.experimental.pallas import tpu as pltpu

N_HEADS = 16
HEAD_DIM = 64
N_EXPERTS = 32
TOP_K = 4
SWIGLU_LIMIT = 7.0
SWIGLU_ALPHA = 1.702
RMS_EPS = 1e-6

LANES = 128
HEADS_PER_STEP = LANES // HEAD_DIM
VMEM_LIMIT_BYTES = 48 * 1024 * 1024
EXPERTS_VMEM_LIMIT_BYTES = 56 * 1024 * 1024

ROW_BLOCK = 256
ATTN_BLOCK = 256
TOKEN_BLOCK = 512
GATHER_BLOCK = 256

SB_SKIP_BELOW = -104.0
DMA_LOOP_UNROLL = 8
ATTN_LANE_GROUPS = 2
FOX_KEY_BLOCK = 1024

ROUTE_IDX, ROUTE_GATE, ROUTE_RANK = 0, TOP_K, 2 * TOP_K

f32 = jnp.float32
bf16 = jnp.bfloat16


def _params(*semantics):
    return pltpu.CompilerParams(dimension_semantics=semantics, vmem_limit_bytes=VMEM_LIMIT_BYTES)


def _rmsnorm(x, g):
    var = jnp.mean(x * x, axis=-1, keepdims=True)
    return x * lax.rsqrt(var + RMS_EPS) * g


def _neg_abs(x):
    bits = lax.bitcast_convert_type(x, jnp.uint32) | jnp.uint32(0x80000000)
    return lax.bitcast_convert_type(bits, f32)


def _split2(x):
    hi = x.astype(bf16)
    lo = (x - hi.astype(f32)).astype(bf16)
    return hi, lo


def _norm_proj_kernel(h_ref, g_ref, w_ref, *rest, with_gate):
    xn = _rmsnorm(h_ref[...], g_ref[...]).astype(bf16)
    if with_gate:
        wf_ref, o_ref, of_ref = rest
        of_ref[...] = jnp.dot(xn, wf_ref[...], preferred_element_type=f32)
    else:
        (o_ref,) = rest
    o_ref[...] = jnp.dot(xn, w_ref[...], preferred_element_type=f32).astype(o_ref.dtype)


def norm_proj(h, g, w, wf=None):
    n, d = h.shape
    m = w.shape[1]
    tm = min(TOKEN_BLOCK, n)
    with_gate = wf is not None
    in_specs = [pl.BlockSpec((tm, d), lambda i: (i, 0)),
                pl.BlockSpec((1, d), lambda i: (0, 0)),
                pl.BlockSpec((d, m), lambda i: (0, 0))]
    out_specs = [pl.BlockSpec((tm, m), lambda i: (i, 0))]
    out_shape = [jax.ShapeDtypeStruct((n, m), bf16)]
    args = [h, g.reshape(1, d), w]
    if with_gate:
        in_specs.append(pl.BlockSpec((d, LANES), lambda i: (0, 0)))
        out_specs.append(pl.BlockSpec((tm, LANES), lambda i: (i, 0)))
        out_shape.append(jax.ShapeDtypeStruct((n, LANES), f32))
        args.append(wf)
    out = pl.pallas_call(
        functools.partial(_norm_proj_kernel, with_gate=with_gate),
        grid=(n // tm,), in_specs=in_specs, out_specs=out_specs, out_shape=out_shape,
        compiler_params=_params("parallel"), name="norm_proj")(*args)
    return out if with_gate else out[0]


def _fox_cumsum_kernel(f_ref, b_ref, tri_ref, c_ref, carry_ref):
    @pl.when(pl.program_id(1) == 0)
    def _():
        carry_ref[...] = jnp.zeros_like(carry_ref)

    x = f_ref[0] + b_ref[...]
    log_f = jnp.minimum(x, 0.0) - jnp.log(1.0 + jnp.exp(-jnp.abs(x)))
    p1 = log_f.astype(bf16)
    r1 = log_f - p1.astype(f32)
    p2 = r1.astype(bf16)
    p3 = (r1 - p2.astype(f32)).astype(bf16)
    tri = tri_ref[...]
    c = (jnp.dot(tri, p1, preferred_element_type=f32)
         + jnp.dot(tri, p2, preferred_element_type=f32)
         + jnp.dot(tri, p3, preferred_element_type=f32)) + carry_ref[...]
    c_ref[0] = c
    carry_ref[...] = c[-1:, :]


def fox_cumsum(f_logit, b_pad, batch, seq):
    ts = min(ATTN_BLOCK, seq)
    tri = (jnp.arange(ts)[None, :] <= jnp.arange(ts)[:, None]).astype(bf16)
    return pl.pallas_call(
        _fox_cumsum_kernel,
        grid=(batch, seq // ts),
        in_specs=[pl.BlockSpec((1, ts, LANES), lambda b, s: (b, s, 0)),
                  pl.BlockSpec((1, LANES), lambda b, s: (0, 0)),
                  pl.BlockSpec((ts, ts), lambda b, s: (0, 0))],
        out_specs=pl.BlockSpec((1, ts, LANES), lambda b, s: (b, s, 0)),
        out_shape=jax.ShapeDtypeStruct((batch, seq, LANES), f32),
        scratch_shapes=[pltpu.VMEM((1, LANES), f32)],
        compiler_params=_params("parallel", "arbitrary"), name="fox_cumsum",
    )(f_logit.reshape(batch, seq, LANES), b_pad, tri)


def _head_mask(hh):
    lane = lax.broadcasted_iota(jnp.int32, (1, LANES), 1)
    return (lane >= hh * HEAD_DIM) & (lane < (hh + 1) * HEAD_DIM)


def _qk(qm, k2):
    return lax.dot_general(qm, k2, (((1,), (1,)), ((), ())), preferred_element_type=f32)


def _sb_attn_kernel(q_ref, k_ref, v_ref, u_ref, o_ref):
    qi = pl.program_id(2)
    tq = q_ref.shape[1]
    tk = tq
    groups = q_ref.shape[2] // LANES
    u2 = u_ref[...]
    masks = [_head_mask(hh) for hh in range(HEADS_PER_STEP)]
    neg_qs = []
    for g in range(groups):
        q2 = q_ref[0, :, g * LANES:(g + 1) * LANES]
        neg_qs += [jnp.where(m, -q2, jnp.zeros_like(q2)) for m in masks]
    n_heads = len(neg_qs)

    def two_blocks(kb, runs, accs, diag):
        has_next = kb >= 1
        kbs = [kb, jnp.maximum(kb - 1, 0)]
        if diag:
            strict = (lax.broadcasted_iota(jnp.int32, (tq, tk), 1)
                      < lax.broadcasted_iota(jnp.int32, (tq, tk), 0))
        chains = [(blk, h) for blk in range(2) for h in range(n_heads)]

        def keys(ref, blk, h):
            g = h // HEADS_PER_STEP
            return ref[0, pl.ds(kbs[blk] * tk, tk), g * LANES:(g + 1) * LANES]

        nzs = {c: _qk(neg_qs[c[1]], keys(k_ref, *c)) for c in chains}
        suffixes = {}
        for c in chains:
            nz = nzs[c]
            log_not = jnp.minimum(nz, 0.0) - jnp.log(1.0 + jnp.exp(_neg_abs(nz)))
            if diag and c[0] == 0:
                log_not = jnp.where(strict, log_not, 0.0)
            hi, lo = _split2(log_not)
            suffixes[c] = jnp.dot(jnp.concatenate([hi, lo], axis=1), u2, preferred_element_type=f32)
        weights, new_runs = {}, []
        for h in range(n_heads):
            after = runs[h] + suffixes[(0, h)][:, 0:1]
            new_runs.append(jnp.where(has_next, after + suffixes[(1, h)][:, 0:1], after))
            for blk, run in ((0, runs[h]), (1, jnp.where(has_next, after, -jnp.inf))):
                a = jnp.exp((run - nzs[(blk, h)]) + suffixes[(blk, h)])
                if diag and blk == 0:
                    a = jnp.where(strict, a, 0.0)
                weights[(blk, h)] = a.astype(bf16)
        new_accs = []
        for g in range(groups):
            hs = range(g * HEADS_PER_STEP, (g + 1) * HEADS_PER_STEP)
            v2s = [keys(v_ref, blk, g * HEADS_PER_STEP) for blk in range(2)]
            vv = jnp.concatenate([jnp.where(m, v2, jnp.zeros_like(v2)) for v2 in v2s for m in masks], axis=0)
            ww = jnp.concatenate([weights[(blk, h)] for blk in range(2) for h in hs], axis=1)
            new_accs.append(accs[g] + jnp.dot(ww, vv, preferred_element_type=f32))
        return new_runs, new_accs

    def alive(runs):
        top = functools.reduce(jnp.maximum, runs)
        return (jnp.max(top) > SB_SKIP_BELOW).astype(jnp.int32)

    zero_run = jnp.zeros((tq, 1), f32)
    runs, accs = two_blocks(qi, [zero_run] * n_heads, [jnp.zeros((tq, LANES), f32)] * groups, True)

    def cond(c):
        return (c[0] >= 0) & (c[1] > 0)

    def body(c):
        runs, accs = two_blocks(c[0], list(c[2]), list(c[3]), False)
        return c[0] - 2, alive(runs), tuple(runs), tuple(accs)

    accs = lax.while_loop(cond, body, (qi - 2, alive(runs), tuple(runs), tuple(accs)))[3]
    o_ref[0] = jnp.concatenate(list(accs), axis=1).astype(o_ref.dtype)


def _fox_attn_kernel(q_ref, k_ref, v_ref, cq_ref, ck_ref, o_ref):
    hp = pl.program_id(1)
    qi = pl.program_id(2)
    tq = q_ref.shape[1]
    tk = min(FOX_KEY_BLOCK, k_ref.shape[1])
    groups = q_ref.shape[2] // LANES
    masks = [_head_mask(hh) for hh in range(HEADS_PER_STEP)]
    q0 = qi * tq
    n_full = q0 // tk
    lane = lax.broadcasted_iota(jnp.int32, (1, LANES), 1)
    c_blk = cq_ref[0]
    qms, cq_ps = [], []
    for g in range(groups):
        q2 = q_ref[0, :, g * LANES:(g + 1) * LANES]
        qms.append([jnp.where(m, q2, jnp.zeros_like(q2)) for m in masks])
        head0 = (hp * groups + g) * HEADS_PER_STEP
        cqs = [jnp.sum(jnp.where(lane == head0 + hh, c_blk, 0.0), axis=1, keepdims=True)
               for hh in range(HEADS_PER_STEP)]
        cq_ps.append(jnp.where(masks[0], cqs[0], cqs[1]))

    def step(k_start, width, masked, carry):
        k_start = pl.multiple_of(k_start, LANES)
        if masked:
            own = (lax.broadcasted_iota(jnp.int32, (tq, tq), 1) <= lax.broadcasted_iota(jnp.int32, (tq, tq), 0))

        def causal(w):
            own_w = jnp.where(own, w[:, width - tq:], -jnp.inf)
            return own_w if width == tq else jnp.concatenate([w[:, :width - tq], own_w], axis=1)
        ws = []
        for g in range(groups):
            k2 = k_ref[0, pl.ds(k_start, width), g * LANES:(g + 1) * LANES]
            for hh in range(HEADS_PER_STEP):
                head = (hp * groups + g) * HEADS_PER_STEP + hh
                ck = ck_ref[0, pl.ds(head, 1), pl.ds(k_start, width)]
                w = _qk(qms[g][hh], k2) - ck
                ws.append(causal(w) if masked else w)
        out = []
        for g in range(groups):
            m_p, l_p, acc = carry[g]
            w0, w1 = ws[g * HEADS_PER_STEP], ws[g * HEADS_PER_STEP + 1]
            mx_p = jnp.where(masks[0], jnp.max(w0, axis=1, keepdims=True), jnp.max(w1, axis=1, keepdims=True))
            m_new = jnp.maximum(m_p, cq_ps[g] + mx_p)
            shift = m_new - cq_ps[g]
            alpha = jnp.exp(m_p - m_new)
            p0 = jnp.exp(w0 - shift[:, 0:1])
            p1 = jnp.exp(w1 - shift[:, HEAD_DIM:HEAD_DIM + 1])
            sum_p = jnp.where(masks[0], jnp.sum(p0, axis=1, keepdims=True), jnp.sum(p1, axis=1, keepdims=True))
            v2 = v_ref[0, pl.ds(k_start, width), g * LANES:(g + 1) * LANES]
            vv = jnp.concatenate([jnp.where(m, v2, jnp.zeros_like(v2)) for m in masks], axis=0)
            acc = acc * alpha + jnp.dot(jnp.concatenate([p0.astype(bf16), p1.astype(bf16)], axis=1), vv,
                                        preferred_element_type=f32)
            out.append((m_new, alpha * l_p + sum_p, acc))
        return tuple(out)

    init = tuple((jnp.full((tq, LANES), -jnp.inf, f32), jnp.zeros((tq, LANES), f32),
                  jnp.zeros((tq, LANES), f32)) for _ in range(groups))
    c = lax.fori_loop(0, n_full, lambda kb, c: step(kb * tk, tk, False, c), init)
    variants = [functools.partial(step, n_full * tk, (j + 1) * tq, True) for j in range(tk // tq)]
    c = variants[0](c) if len(variants) == 1 else lax.switch((q0 - n_full * tk) // tq, variants, c)
    o_ref[0] = jnp.concatenate([acc / l_p for _, l_p, acc in c], axis=1).astype(o_ref.dtype)


def attention(qkv, batch, seq, c=None):
    d = qkv.shape[1] // 3
    width = LANES * ATTN_LANE_GROUPS
    nlb = d // width
    tq = min(ATTN_BLOCK, seq)
    qkv3 = qkv.reshape(batch, seq, 3 * d)
    in_specs = [pl.BlockSpec((1, tq, width), lambda b, hp, qi: (b, qi, hp)),
                pl.BlockSpec((1, seq, width), lambda b, hp, qi: (b, 0, nlb + hp)),
                pl.BlockSpec((1, seq, width), lambda b, hp, qi: (b, 0, 2 * nlb + hp))]
    args = [qkv3, qkv3, qkv3]
    if c is None:
        kern = _sb_attn_kernel
        u = (jnp.arange(tq)[:, None] >= jnp.arange(tq)[None, :]).astype(bf16)
        in_specs.append(pl.BlockSpec((2 * tq, tq), lambda b, hp, qi: (0, 0)))
        args.append(jnp.concatenate([u, u], axis=0))
    else:
        kern = _fox_attn_kernel
        tk = min(FOX_KEY_BLOCK, seq)
        assert tk % tq == 0 and seq % tk == 0
        ch = jnp.transpose(c[:, :, :N_HEADS], (0, 2, 1))
        in_specs.append(pl.BlockSpec((1, tq, LANES), lambda b, hp, qi: (b, qi, 0)))
        in_specs.append(pl.BlockSpec((1, N_HEADS, seq), lambda b, hp, qi: (b, 0, 0)))
        args += [c, ch]
    out = pl.pallas_call(
        kern, grid=(batch, nlb, seq // tq), in_specs=in_specs,
        out_specs=pl.BlockSpec((1, tq, width), lambda b, hp, qi: (b, qi, hp)),
        out_shape=jax.ShapeDtypeStruct((batch, seq, d), bf16),
        compiler_params=_params("parallel", "parallel", "arbitrary"), name="attention")(*args)
    return out.reshape(batch * seq, d)


def _oproj_router_kernel(o_ref, h_ref, wo_ref, g_ref, wrh_ref, wrl_ref, br_ref, tri_ref,
                         h1_ref, xn_ref, route_ref, cnt_ref, carry_ref):
    @pl.when(pl.program_id(0) == 0)
    def _():
        carry_ref[...] = jnp.zeros_like(carry_ref)

    tm = h_ref.shape[0]
    h1 = h_ref[...] + jnp.dot(o_ref[...], wo_ref[...], preferred_element_type=f32)
    h1_ref[...] = h1
    xn = _rmsnorm(h1, g_ref[...])
    xn_ref[...] = xn

    xh, xl = _split2(xn)
    logits = (jnp.dot(xh, wrh_ref[...], preferred_element_type=f32)
              + jnp.dot(xh, wrl_ref[...], preferred_element_type=f32)
              + jnp.dot(xl, wrh_ref[...], preferred_element_type=f32)) + br_ref[...]
    lane = lax.broadcasted_iota(jnp.int32, (tm, LANES), 1)
    lane_f = lane.astype(f32)
    work = jnp.where(lane < N_EXPERTS, logits, -jnp.inf)

    vals, idxs, hots = [], [], []
    for _ in range(TOP_K):
        mx = jnp.max(work, axis=1, keepdims=True)
        idx = jnp.min(jnp.where(work == mx, lane_f, float(LANES)), axis=1, keepdims=True)
        hot = lane_f == idx
        work = jnp.where(hot, -jnp.inf, work)
        vals.append(mx)
        idxs.append(idx)
        hots.append(hot)

    exps = [jnp.exp(v - vals[0]) for v in vals]
    denom = exps[0] + exps[1] + exps[2] + exps[3]
    gates = [e / denom for e in exps]

    sel = jnp.zeros((tm, LANES), f32)
    for hot in hots:
        sel = sel + hot.astype(f32)
    before = jnp.dot(tri_ref[...], sel.astype(bf16), preferred_element_type=f32) + carry_ref[...]
    ranks = [jnp.sum(jnp.where(hot, before, 0.0), axis=1, keepdims=True) for hot in hots]
    carry = carry_ref[...] + jnp.sum(sel, axis=0, keepdims=True)
    carry_ref[...] = carry
    cnt_ref[...] = jnp.broadcast_to(carry, cnt_ref.shape)

    route = jnp.zeros((tm, LANES), f32)
    for base, cols in ((ROUTE_IDX, idxs), (ROUTE_GATE, gates), (ROUTE_RANK, ranks)):
        for k, colv in enumerate(cols):
            route = jnp.where(lane == base + k, colv, route)
    route_ref[...] = route


def oproj_router(o, h, wo, g, wr_hi, wr_lo, br):
    n, d = h.shape
    tm = min(TOKEN_BLOCK, n)
    tri = (jnp.arange(tm)[None, :] < jnp.arange(tm)[:, None]).astype(bf16)
    row = lambda i: (i, 0)
    fixed = lambda i: (0, 0)
    return pl.pallas_call(
        _oproj_router_kernel,
        grid=(n // tm,),
        in_specs=[pl.BlockSpec((tm, d), row), pl.BlockSpec((tm, d), row),
                  pl.BlockSpec((d, d), fixed), pl.BlockSpec((1, d), fixed),
                  pl.BlockSpec((d, LANES), fixed), pl.BlockSpec((d, LANES), fixed),
                  pl.BlockSpec((1, LANES), fixed), pl.BlockSpec((tm, tm), fixed)],
        out_specs=[pl.BlockSpec((tm, d), row), pl.BlockSpec((tm, d), row),
                   pl.BlockSpec((tm, LANES), row), pl.BlockSpec((8, LANES), fixed)],
        out_shape=[jax.ShapeDtypeStruct((n, d), f32), jax.ShapeDtypeStruct((n, d), f32),
                   jax.ShapeDtypeStruct((n, LANES), f32), jax.ShapeDtypeStruct((8, LANES), f32)],
        scratch_shapes=[pltpu.VMEM((1, LANES), f32)],
        compiler_params=_params("arbitrary"), name="oproj_router",
    )(o, h, wo, g.reshape(1, d), wr_hi, wr_lo, br, tri)


def _dispatch_kernel(zlo_ref, zhi_ref, pos_ref, x_ref, rows_ref, zero_ref, sem):
    tm = x_ref.shape[0]

    def row_copy(src, r):
        return pltpu.make_async_copy(src, rows_ref.at[pl.ds(r, 1)], sem)

    @pl.when(pl.program_id(0) == 0)
    def _():
        zero_ref[...] = jnp.zeros_like(zero_ref)
        for wait in (False, True):
            def per_expert(e, _, wait=wait):
                def per_row(r, _):
                    cp = row_copy(zero_ref, r)
                    cp.wait() if wait else cp.start()
                    return 0
                return lax.fori_loop(zlo_ref[e], zhi_ref[e], per_row, 0)
            lax.fori_loop(0, N_EXPERTS, per_expert, 0)

    for wait in (False, True):
        def per_token(t, _, wait=wait):
            for k in range(TOP_K):
                cp = row_copy(x_ref.at[pl.ds(t, 1)], pos_ref[t * TOP_K + k])
                cp.wait() if wait else cp.start(priority=k % 2)
            return 0
        lax.fori_loop(0, tm, per_token, 0, unroll=DMA_LOOP_UNROLL)


def dispatch(xn, pos_flat, zero_lo, zero_hi, n_rows):
    n, d = xn.shape
    tm = min(GATHER_BLOCK, n)
    grid_spec = pltpu.PrefetchScalarGridSpec(
        num_scalar_prefetch=2, grid=(n // tm,),
        in_specs=[pl.BlockSpec((tm * TOP_K,), lambda i, *_: (i,), memory_space=pltpu.SMEM),
                  pl.BlockSpec((tm, d), lambda i, *_: (i, 0))],
        out_specs=pl.BlockSpec(memory_space=pl.ANY),
        scratch_shapes=[pltpu.VMEM((1, d), f32), pltpu.SemaphoreType.DMA(())])
    return pl.pallas_call(
        _dispatch_kernel, grid_spec=grid_spec,
        out_shape=jax.ShapeDtypeStruct((n_rows, d), f32),
        compiler_params=_params("arbitrary"), name="dispatch",
    )(zero_lo, zero_hi, pos_flat, xn)


def _experts_kernel(be_ref, nused_ref, x_ref, wgu_ref, bgu_ref, wd_ref, bd_ref, y_ref, wgu_b, wd_b):
    rb = pl.program_id(0)
    d_ff = wd_ref.shape[2]
    used = rb < nused_ref[0]
    new_expert = (rb == 0) | (be_ref[rb] != be_ref[jnp.maximum(rb - 1, 0)])

    @pl.when(used & new_expert)
    def _():
        wgu_b[...] = wgu_ref[0, 0].astype(bf16)
        wd_b[...] = wd_ref[0, 0].astype(bf16)

    @pl.when(used)
    def _():
        x = x_ref[...].astype(bf16)
        hgu = jnp.dot(x, wgu_b[...], preferred_element_type=f32) + bgu_ref[0, 0]
        glu = jnp.minimum(hgu[:, :d_ff], SWIGLU_LIMIT)
        lin = jnp.clip(hgu[:, d_ff:], -SWIGLU_LIMIT, SWIGLU_LIMIT)
        act = glu * (1.0 / (1.0 + jnp.exp(-SWIGLU_ALPHA * glu))) * (lin + 1.0)
        y_ref[...] = jnp.dot(act.astype(bf16), wd_b[...], preferred_element_type=f32) + bd_ref[0, 0]

    @pl.when(jnp.logical_not(used))
    def _():
        y_ref[...] = jnp.zeros_like(y_ref)


def experts(x_rows, block_expert, n_used, layer, w_gu, b_gu, w_down, b_down):
    n_rows, d = x_rows.shape
    f2 = w_gu.shape[3]
    d_ff = f2 // 2
    wmap = lambda rb, be, nu: (layer, be[rb], 0, 0)
    rows = pl.BlockSpec((ROW_BLOCK, d), lambda rb, be, nu: (rb, 0))
    grid_spec = pltpu.PrefetchScalarGridSpec(
        num_scalar_prefetch=2, grid=(n_rows // ROW_BLOCK,),
        in_specs=[rows, pl.BlockSpec((1, 1, d, f2), wmap), pl.BlockSpec((1, 1, 1, f2), wmap),
                  pl.BlockSpec((1, 1, d_ff, d), wmap), pl.BlockSpec((1, 1, 1, d), wmap)],
        out_specs=rows,
        scratch_shapes=[pltpu.VMEM((d, f2), bf16), pltpu.VMEM((d_ff, d), bf16)])
    return pl.pallas_call(
        _experts_kernel, grid_spec=grid_spec,
        out_shape=jax.ShapeDtypeStruct((n_rows, d), f32),
        compiler_params=pltpu.CompilerParams(dimension_semantics=("arbitrary",),
                                             vmem_limit_bytes=EXPERTS_VMEM_LIMIT_BYTES),
        name="experts",
    )(block_expert, n_used, x_rows, w_gu, b_gu, w_down, b_down)


def _combine_ple_kernel(pos_ref, pos_next_ref, route_ref, h_ref, p_ref, g_ref, wg_ref, wp_ref, gf_ref,
                        yrows_ref, o_ref, ybuf0_ref, ybuf1_ref, sems, *, final):
    i = pl.program_id(0)
    tm = h_ref.shape[0]

    def row_copy(idx_ref, buf, s, t, k):
        r = 0 if idx_ref is None else idx_ref[t * TOP_K + k]
        return pltpu.make_async_copy(yrows_ref.at[pl.ds(r, 1)], buf.at[k, pl.ds(t, 1)], sems.at[s])

    def start_rows(idx_ref, buf, s, tokens):
        for t in tokens:
            for k in range(TOP_K):
                row_copy(idx_ref, buf, s, t, k).start(priority=k % 2)

    def wait_rows(buf, s):
        def per_token(t, _):
            for k in range(TOP_K):
                row_copy(None, buf, s, t, k).wait()
            return 0
        lax.fori_loop(0, tm, per_token, 0, unroll=DMA_LOOP_UNROLL)

    @pl.when(i == 0)
    def _():
        def per_token(t, _):
            start_rows(pos_ref, ybuf0_ref, 0, [t])
            return 0
        lax.fori_loop(0, tm, per_token, 0, unroll=DMA_LOOP_UNROLL)

    def step(cur, nxt, s_cur, s_nxt):
        wait_rows(cur, s_cur)
        start_rows(pos_next_ref, nxt, s_nxt, range(tm))
        proj = jnp.dot(p_ref[...].astype(bf16), wp_ref[...], preferred_element_type=f32)
        route = route_ref[...]
        h2 = h_ref[...]
        for k in range(TOP_K):
            h2 = h2 + route[:, ROUTE_GATE + k:ROUTE_GATE + k + 1] * cur[k]
        hn = _rmsnorm(h2, g_ref[...]).astype(bf16)
        gate = 1.0 / (1.0 + jnp.exp(-jnp.dot(hn, wg_ref[...], preferred_element_type=f32)))
        h3 = h2 + gate * proj
        if final:
            h3 = _rmsnorm(h3, gf_ref[...])
        o_ref[...] = h3

        @pl.when(i == pl.num_programs(0) - 1)
        def _():
            wait_rows(nxt, s_nxt)

    @pl.when(i % 2 == 0)
    def _():
        step(ybuf0_ref, ybuf1_ref, 0, 1)

    @pl.when(i % 2 == 1)
    def _():
        step(ybuf1_ref, ybuf0_ref, 1, 0)


def combine_ple(pos_flat, route, h1, y_rows, p, g, wg, wp, g_final, final):
    n, d = h1.shape
    pd = p.shape[1]
    tm = min(GATHER_BLOCK, n)
    last = n // tm - 1
    row = lambda i: (i, 0)
    fixed = lambda i: (0, 0)
    return pl.pallas_call(
        functools.partial(_combine_ple_kernel, final=final),
        grid=(n // tm,),
        in_specs=[pl.BlockSpec((tm * TOP_K,), lambda i: (i,), memory_space=pltpu.SMEM),
                  pl.BlockSpec((tm * TOP_K,), lambda i: (jnp.minimum(i + 1, last),), memory_space=pltpu.SMEM),
                  pl.BlockSpec((tm, LANES), row), pl.BlockSpec((tm, d), row),
                  pl.BlockSpec((tm, pd), row), pl.BlockSpec((1, d), fixed),
                  pl.BlockSpec((d, d), fixed), pl.BlockSpec((pd, d), fixed),
                  pl.BlockSpec((1, d), fixed), pl.BlockSpec(memory_space=pl.ANY)],
        out_specs=pl.BlockSpec((tm, d), row),
        out_shape=jax.ShapeDtypeStruct((n, d), f32),
        scratch_shapes=[pltpu.VMEM((TOP_K, tm, d), f32), pltpu.VMEM((TOP_K, tm, d), f32),
                        pltpu.SemaphoreType.DMA((2,))],
        compiler_params=_params("arbitrary"), name="combine_ple",
    )(pos_flat, pos_flat, route, h1, p, g.reshape(1, d), wg, wp, g_final.reshape(1, d), y_rows)


def _routing_tables(route, counts_f, n):
    counts = counts_f[0, :N_EXPERTS].astype(jnp.int32)
    padded = (counts + ROW_BLOCK - 1) // ROW_BLOCK * ROW_BLOCK
    pad_end = jnp.cumsum(padded)
    pad_start = pad_end - padded
    n_blocks = -(-(n * TOP_K + N_EXPERTS * (ROW_BLOCK - 1)) // ROW_BLOCK)
    n_rows = n_blocks * ROW_BLOCK
    idx = route[:, ROUTE_IDX:ROUTE_IDX + TOP_K].astype(jnp.int32)
    rank = route[:, ROUTE_RANK:ROUTE_RANK + TOP_K].astype(jnp.int32)
    pos_flat = (pad_start[idx] + rank).reshape(-1)
    block_start = jnp.arange(n_blocks, dtype=jnp.int32) * ROW_BLOCK
    block_expert = jnp.minimum(jnp.sum(block_start[:, None] >= pad_end[None, :], axis=1),
                               N_EXPERTS - 1).astype(jnp.int32)
    n_used = (pad_end[-1:] // ROW_BLOCK).astype(jnp.int32)
    zero_lo = (pad_start + counts).astype(jnp.int32)
    zero_hi = jnp.concatenate([pad_start[1:], jnp.array([n_rows], jnp.int32)]).astype(jnp.int32)
    return pos_flat, block_expert, n_used, zero_lo, zero_hi, n_rows


def kernel(x, p, g_mix, w_qkv, w_fgate, b_fgate, w_o, g_moe, w_router, b_router, w_gu, b_gu,
           w_down, b_down, g_ple, w_ple_gate, w_ple_proj, g_final):
    batch, seq, d = x.shape
    depth = w_qkv.shape[0]
    n = batch * seq
    h = x.reshape(n, d)
    q_scale = jnp.concatenate([jnp.full((d,), HEAD_DIM ** -0.5, f32), jnp.ones((2 * d,), f32)])
    n_exp = w_gu.shape[1]
    b_gu4 = b_gu.reshape(depth, n_exp, 1, -1)
    b_down4 = b_down.reshape(depth, n_exp, 1, -1)
    for i in range(depth):
        wqkv = (w_qkv[i] * q_scale).astype(bf16)
        if i % 2 == 0:
            qkv = norm_proj(h, g_mix[i], wqkv)
            o = attention(qkv, batch, seq)
        else:
            j = i // 2
            wf = jnp.pad(w_fgate[j], ((0, 0), (0, LANES - N_HEADS))).astype(bf16)
            bf = jnp.pad(b_fgate[j], (0, LANES - N_HEADS)).reshape(1, LANES)
            qkv, f_logit = norm_proj(h, g_mix[i], wqkv, wf)
            c = fox_cumsum(f_logit, bf, batch, seq)
            o = attention(qkv, batch, seq, c)

        wr = jnp.pad(w_router[i], ((0, 0), (0, LANES - N_EXPERTS)))
        wr_hi = wr.astype(bf16)
        wr_lo = (wr - wr_hi.astype(f32)).astype(bf16)
        br = jnp.pad(b_router[i], (0, LANES - N_EXPERTS)).reshape(1, LANES)
        h1, xn, route, counts = oproj_router(o, h, w_o[i].astype(bf16), g_moe[i], wr_hi, wr_lo, br)

        pos_flat, block_expert, n_used, zero_lo, zero_hi, n_rows = _routing_tables(route, counts, n)
        x_rows = dispatch(xn, pos_flat, zero_lo, zero_hi, n_rows)
        y_rows = experts(x_rows, block_expert, n_used, i, w_gu, b_gu4, w_down, b_down4)
        h = combine_ple(pos_flat, route, h1, y_rows, p[i].reshape(n, -1), g_ple[i],
                        w_ple_gate[i].astype(bf16), w_ple_proj[i].astype(bf16), g_final,
                        final=(i == depth - 1))
    return h.reshape(batch, seq, d)
```

```python
import functools

import jax
import jax.numpy as jnp
from jax import lax
from jax.experimental import pallas as pl
from jax.experimental.pallas import tpu as pltpu

N_HEADS = 16
HEAD_DIM = 64
N_EXPERTS = 32
TOP_K = 4
SWIGLU_LIMIT = 7.0
SWIGLU_ALPHA = 1.702
RMS_EPS = 1e-6

LANES = 128
HEADS_PER_STEP = LANES // HEAD_DIM
VMEM_LIMIT_BYTES = 48 * 1024 * 1024
EXPERTS_VMEM_LIMIT_BYTES = 56 * 1024 * 1024

ROW_BLOCK = 512
ATTN_BLOCK = 256
TOKEN_BLOCK = 512
GATHER_BLOCK = 256

SB_SKIP_BELOW = -104.0
DMA_LOOP_UNROLL = 8
ATTN_LANE_GROUPS = 2
FOX_KEY_BLOCK = 2048

ROUTE_IDX, ROUTE_GATE, ROUTE_RANK = 0, TOP_K, 2 * TOP_K

f32 = jnp.float32
bf16 = jnp.bfloat16


def _params(*semantics):
    return pltpu.CompilerParams(dimension_semantics=semantics, vmem_limit_bytes=VMEM_LIMIT_BYTES)


def _rmsnorm(x, g):
    var = jnp.mean(x * x, axis=-1, keepdims=True)
    return x * lax.rsqrt(var + RMS_EPS) * g


def _neg_abs(x):
    bits = lax.bitcast_convert_type(x, jnp.uint32) | jnp.uint32(0x80000000)
    return lax.bitcast_convert_type(bits, f32)


def _split2(x):
    hi = x.astype(bf16)
    lo = (x - hi.astype(f32)).astype(bf16)
    return hi, lo


def _norm_proj_kernel(h_ref, g_ref, w_ref, *rest, with_gate):
    xn = _rmsnorm(h_ref[...], g_ref[...]).astype(bf16)
    if with_gate:
        wf_ref, o_ref, of_ref = rest
        of_ref[...] = jnp.dot(xn, wf_ref[...], preferred_element_type=f32)
    else:
        (o_ref,) = rest
    o_ref[...] = jnp.dot(xn, w_ref[...], preferred_element_type=f32).astype(o_ref.dtype)


def norm_proj(h, g, w, wf=None):
    n, d = h.shape
    m = w.shape[1]
    tm = min(TOKEN_BLOCK, n)
    with_gate = wf is not None
    in_specs = [pl.BlockSpec((tm, d), lambda i: (i, 0)),
                pl.BlockSpec((1, d), lambda i: (0, 0)),
                pl.BlockSpec((d, m), lambda i: (0, 0))]
    out_specs = [pl.BlockSpec((tm, m), lambda i: (i, 0))]
    out_shape = [jax.ShapeDtypeStruct((n, m), bf16)]
    args = [h, g.reshape(1, d), w]
    if with_gate:
        in_specs.append(pl.BlockSpec((d, LANES), lambda i: (0, 0)))
        out_specs.append(pl.BlockSpec((tm, LANES), lambda i: (i, 0)))
        out_shape.append(jax.ShapeDtypeStruct((n, LANES), f32))
        args.append(wf)
    out = pl.pallas_call(
        functools.partial(_norm_proj_kernel, with_gate=with_gate),
        grid=(n // tm,), in_specs=in_specs, out_specs=out_specs, out_shape=out_shape,
        compiler_params=_params("parallel"), name="norm_proj")(*args)
    return out if with_gate else out[0]


def _fox_cumsum_kernel(f_ref, b_ref, tri_ref, c_ref, carry_ref):
    @pl.when(pl.program_id(1) == 0)
    def _():
        carry_ref[...] = jnp.zeros_like(carry_ref)

    x = f_ref[0] + b_ref[...]
    log_f = jnp.minimum(x, 0.0) - jnp.log(1.0 + jnp.exp(-jnp.abs(x)))
    p1 = log_f.astype(bf16)
    r1 = log_f - p1.astype(f32)
    p2 = r1.astype(bf16)
    p3 = (r1 - p2.astype(f32)).astype(bf16)
    tri = tri_ref[...]
    c = (jnp.dot(tri, p1, preferred_element_type=f32)
         + jnp.dot(tri, p2, preferred_element_type=f32)
         + jnp.dot(tri, p3, preferred_element_type=f32)) + carry_ref[...]
    c_ref[0] = c
    carry_ref[...] = c[-1:, :]


def fox_cumsum(f_logit, b_pad, batch, seq):
    ts = min(ATTN_BLOCK, seq)
    tri = (jnp.arange(ts)[None, :] <= jnp.arange(ts)[:, None]).astype(bf16)
    return pl.pallas_call(
        _fox_cumsum_kernel,
        grid=(batch, seq // ts),
        in_specs=[pl.BlockSpec((1, ts, LANES), lambda b, s: (b, s, 0)),
                  pl.BlockSpec((1, LANES), lambda b, s: (0, 0)),
                  pl.BlockSpec((ts, ts), lambda b, s: (0, 0))],
        out_specs=pl.BlockSpec((1, ts, LANES), lambda b, s: (b, s, 0)),
        out_shape=jax.ShapeDtypeStruct((batch, seq, LANES), f32),
        scratch_shapes=[pltpu.VMEM((1, LANES), f32)],
        compiler_params=_params("parallel", "arbitrary"), name="fox_cumsum",
    )(f_logit.reshape(batch, seq, LANES), b_pad, tri)


def _head_mask(hh):
    lane = lax.broadcasted_iota(jnp.int32, (1, LANES), 1)
    return (lane >= hh * HEAD_DIM) & (lane < (hh + 1) * HEAD_DIM)


def _qk(qm, k2):
    return lax.dot_general(qm, k2, (((1,), (1,)), ((), ())), preferred_element_type=f32)


def _sb_attn_kernel(q_ref, k_ref, v_ref, u_ref, o_ref):
    qi = pl.program_id(2)
    tq = q_ref.shape[1]
    tk = tq
    groups = q_ref.shape[2] // LANES
    u2 = u_ref[...]
    masks = [_head_mask(hh) for hh in range(HEADS_PER_STEP)]
    neg_qs = []
    for g in range(groups):
        q2 = q_ref[0, :, g * LANES:(g + 1) * LANES]
        neg_qs += [jnp.where(m, -q2, jnp.zeros_like(q2)) for m in masks]
    n_heads = len(neg_qs)

    def two_blocks(kb, runs, accs, diag):
        has_next = kb >= 1
        kbs = [kb, jnp.maximum(kb - 1, 0)]
        if diag:
            strict = (lax.broadcasted_iota(jnp.int32, (tq, tk), 1)
                      < lax.broadcasted_iota(jnp.int32, (tq, tk), 0))
        chains = [(blk, h) for blk in range(2) for h in range(n_heads)]

        def keys(ref, blk, h):
            g = h // HEADS_PER_STEP
            return ref[0, pl.ds(kbs[blk] * tk, tk), g * LANES:(g + 1) * LANES]

        nzs = {c: _qk(neg_qs[c[1]], keys(k_ref, *c)) for c in chains}
        suffixes = {}
        for c in chains:
            nz = nzs[c]
            log_not = jnp.minimum(nz, 0.0) - jnp.log(1.0 + jnp.exp(_neg_abs(nz)))
            if diag and c[0] == 0:
                log_not = jnp.where(strict, log_not, 0.0)
            hi, lo = _split2(log_not)
            suffixes[c] = jnp.dot(jnp.concatenate([hi, lo], axis=1), u2, preferred_element_type=f32)
        weights, new_runs = {}, []
        for h in range(n_heads):
            after = runs[h] + suffixes[(0, h)][:, 0:1]
            new_runs.append(jnp.where(has_next, after + suffixes[(1, h)][:, 0:1], after))
            for blk, run in ((0, runs[h]), (1, jnp.where(has_next, after, -jnp.inf))):
                a = jnp.exp((run - nzs[(blk, h)]) + suffixes[(blk, h)])
                if diag and blk == 0:
                    a = jnp.where(strict, a, 0.0)
                weights[(blk, h)] = a.astype(bf16)
        new_accs = []
        for g in range(groups):
            hs = range(g * HEADS_PER_STEP, (g + 1) * HEADS_PER_STEP)
            v2s = [keys(v_ref, blk, g * HEADS_PER_STEP) for blk in range(2)]
            vv = jnp.concatenate([jnp.where(m, v2, jnp.zeros_like(v2)) for v2 in v2s for m in masks], axis=0)
            ww = jnp.concatenate([weights[(blk, h)] for blk in range(2) for h in hs], axis=1)
            new_accs.append(accs[g] + jnp.dot(ww, vv, preferred_element_type=f32))
        return new_runs, new_accs

    def alive(runs):
        top = functools.reduce(jnp.maximum, runs)
        return (jnp.max(top) > SB_SKIP_BELOW).astype(jnp.int32)

    zero_run = jnp.zeros((tq, 1), f32)
    runs, accs = two_blocks(qi, [zero_run] * n_heads, [jnp.zeros((tq, LANES), f32)] * groups, True)

    def cond(c):
        return (c[0] >= 0) & (c[1] > 0)

    def body(c):
        runs, accs = two_blocks(c[0], list(c[2]), list(c[3]), False)
        return c[0] - 2, alive(runs), tuple(runs), tuple(accs)

    accs = lax.while_loop(cond, body, (qi - 2, alive(runs), tuple(runs), tuple(accs)))[3]
    o_ref[0] = jnp.concatenate(list(accs), axis=1).astype(o_ref.dtype)


def _fox_attn_kernel(q_ref, k_ref, v_ref, cq_ref, ck_ref, o_ref):
    hp = pl.program_id(1)
    qi = pl.program_id(2)
    tq = q_ref.shape[1]
    tk = min(FOX_KEY_BLOCK, k_ref.shape[1])
    groups = q_ref.shape[2] // LANES
    masks = [_head_mask(hh) for hh in range(HEADS_PER_STEP)]
    q0 = qi * tq
    n_full = q0 // tk
    lane = lax.broadcasted_iota(jnp.int32, (1, LANES), 1)
    c_blk = cq_ref[0]
    qms, cq_ps = [], []
    for g in range(groups):
        q2 = q_ref[0, :, g * LANES:(g + 1) * LANES]
        qms.append([jnp.where(m, q2, jnp.zeros_like(q2)) for m in masks])
        head0 = (hp * groups + g) * HEADS_PER_STEP
        cqs = [jnp.sum(jnp.where(lane == head0 + hh, c_blk, 0.0), axis=1, keepdims=True)
               for hh in range(HEADS_PER_STEP)]
        cq_ps.append(jnp.where(masks[0], cqs[0], cqs[1]))

    def step(k_start, width, masked, carry):
        k_start = pl.multiple_of(k_start, LANES)
        if masked:
            own = (lax.broadcasted_iota(jnp.int32, (tq, tq), 1) <= lax.broadcasted_iota(jnp.int32, (tq, tq), 0))

        def causal(w):
            own_w = jnp.where(own, w[:, width - tq:], -jnp.inf)
            return own_w if width == tq else jnp.concatenate([w[:, :width - tq], own_w], axis=1)
        ws = []
        for g in range(groups):
            k2 = k_ref[0, pl.ds(k_start, width), g * LANES:(g + 1) * LANES]
            for hh in range(HEADS_PER_STEP):
                head = (hp * groups + g) * HEADS_PER_STEP + hh
                ck = ck_ref[0, pl.ds(head, 1), pl.ds(k_start, width)]
                w = _qk(qms[g][hh], k2) - ck
                ws.append(causal(w) if masked else w)
        out = []
        for g in range(groups):
            m_p, l_p, acc = carry[g]
            w0, w1 = ws[g * HEADS_PER_STEP], ws[g * HEADS_PER_STEP + 1]
            mx_p = jnp.where(masks[0], jnp.max(w0, axis=1, keepdims=True), jnp.max(w1, axis=1, keepdims=True))
            m_new = jnp.maximum(m_p, cq_ps[g] + mx_p)
            shift = m_new - cq_ps[g]
            alpha = jnp.exp(m_p - m_new)
            p0 = jnp.exp(w0 - shift[:, 0:1])
            p1 = jnp.exp(w1 - shift[:, HEAD_DIM:HEAD_DIM + 1])
            sum_p = jnp.where(masks[0], jnp.sum(p0, axis=1, keepdims=True), jnp.sum(p1, axis=1, keepdims=True))
            v2 = v_ref[0, pl.ds(k_start, width), g * LANES:(g + 1) * LANES]
            vv = jnp.concatenate([jnp.where(m, v2, jnp.zeros_like(v2)) for m in masks], axis=0)
            acc = acc * alpha + jnp.dot(jnp.concatenate([p0.astype(bf16), p1.astype(bf16)], axis=1), vv,
                                        preferred_element_type=f32)
            out.append((m_new, alpha * l_p + sum_p, acc))
        return tuple(out)

    init = tuple((jnp.full((tq, LANES), -jnp.inf, f32), jnp.zeros((tq, LANES), f32),
                  jnp.zeros((tq, LANES), f32)) for _ in range(groups))
    c = init if tk == k_ref.shape[1] else lax.fori_loop(0, n_full, lambda kb, c: step(kb * tk, tk, False, c), init)
    variants = [functools.partial(step, n_full * tk, (j + 1) * tq, True) for j in range(tk // tq)]
    c = variants[0](c) if len(variants) == 1 else lax.switch((q0 - n_full * tk) // tq, variants, c)
    o_ref[0] = jnp.concatenate([acc / l_p for _, l_p, acc in c], axis=1).astype(o_ref.dtype)


def attention(qkv, batch, seq, c=None):
    d = qkv.shape[1] // 3
    width = LANES * ATTN_LANE_GROUPS
    nlb = d // width
    tq = min(ATTN_BLOCK, seq)
    qkv3 = qkv.reshape(batch, seq, 3 * d)
    in_specs = [pl.BlockSpec((1, tq, width), lambda b, hp, qi: (b, qi, hp)),
                pl.BlockSpec((1, seq, width), lambda b, hp, qi: (b, 0, nlb + hp)),
                pl.BlockSpec((1, seq, width), lambda b, hp, qi: (b, 0, 2 * nlb + hp))]
    args = [qkv3, qkv3, qkv3]
    if c is None:
        kern = _sb_attn_kernel
        u = (jnp.arange(tq)[:, None] >= jnp.arange(tq)[None, :]).astype(bf16)
        in_specs.append(pl.BlockSpec((2 * tq, tq), lambda b, hp, qi: (0, 0)))
        args.append(jnp.concatenate([u, u], axis=0))
    else:
        kern = _fox_attn_kernel
        tk = min(FOX_KEY_BLOCK, seq)
        assert tk % tq == 0 and seq % tk == 0
        ch = jnp.transpose(c[:, :, :N_HEADS], (0, 2, 1))
        in_specs.append(pl.BlockSpec((1, tq, LANES), lambda b, hp, qi: (b, qi, 0)))
        in_specs.append(pl.BlockSpec((1, N_HEADS, seq), lambda b, hp, qi: (b, 0, 0)))
        args += [c, ch]
    out = pl.pallas_call(
        kern, grid=(batch, nlb, seq // tq), in_specs=in_specs,
        out_specs=pl.BlockSpec((1, tq, width), lambda b, hp, qi: (b, qi, hp)),
        out_shape=jax.ShapeDtypeStruct((batch, seq, d), bf16),
        compiler_params=_params("parallel", "parallel", "arbitrary"), name="attention")(*args)
    return out.reshape(batch * seq, d)


def _oproj_router_kernel(o_ref, h_ref, wo_ref, g_ref, wrh_ref, wrl_ref, br_ref, tri_ref,
                         h1_ref, xn_ref, route_ref, cnt_ref, carry_ref):
    @pl.when(pl.program_id(0) == 0)
    def _():
        carry_ref[...] = jnp.zeros_like(carry_ref)

    tm = h_ref.shape[0]
    h1 = h_ref[...] + jnp.dot(o_ref[...], wo_ref[...], preferred_element_type=f32)
    h1_ref[...] = h1
    xn = _rmsnorm(h1, g_ref[...])
    xn_ref[...] = xn

    xh, xl = _split2(xn)
    logits = (jnp.dot(xh, wrh_ref[...], preferred_element_type=f32)
              + jnp.dot(xh, wrl_ref[...], preferred_element_type=f32)
              + jnp.dot(xl, wrh_ref[...], preferred_element_type=f32)) + br_ref[...]
    lane = lax.broadcasted_iota(jnp.int32, (tm, LANES), 1)
    lane_f = lane.astype(f32)
    work = jnp.where(lane < N_EXPERTS, logits, -jnp.inf)

    vals, idxs, hots = [], [], []
    for _ in range(TOP_K):
        mx = jnp.max(work, axis=1, keepdims=True)
        idx = jnp.min(jnp.where(work == mx, lane_f, float(LANES)), axis=1, keepdims=True)
        hot = lane_f == idx
        work = jnp.where(hot, -jnp.inf, work)
        vals.append(mx)
        idxs.append(idx)
        hots.append(hot)

    exps = [jnp.exp(v - vals[0]) for v in vals]
    denom = exps[0] + exps[1] + exps[2] + exps[3]
    gates = [e / denom for e in exps]

    sel = jnp.zeros((tm, LANES), f32)
    for hot in hots:
        sel = sel + hot.astype(f32)
    before = jnp.dot(tri_ref[...], sel.astype(bf16), preferred_element_type=f32) + carry_ref[...]
    ranks = [jnp.sum(jnp.where(hot, before, 0.0), axis=1, keepdims=True) for hot in hots]
    carry = carry_ref[...] + jnp.sum(sel, axis=0, keepdims=True)
    carry_ref[...] = carry
    cnt_ref[...] = jnp.broadcast_to(carry, cnt_ref.shape)

    route = jnp.zeros((tm, LANES), f32)
    for base, cols in ((ROUTE_IDX, idxs), (ROUTE_GATE, gates), (ROUTE_RANK, ranks)):
        for k, colv in enumerate(cols):
            route = jnp.where(lane == base + k, colv, route)
    route_ref[...] = route


def oproj_router(o, h, wo, g, wr_hi, wr_lo, br):
    n, d = h.shape
    tm = min(TOKEN_BLOCK, n)
    tri = (jnp.arange(tm)[None, :] < jnp.arange(tm)[:, None]).astype(bf16)
    row = lambda i: (i, 0)
    fixed = lambda i: (0, 0)
    return pl.pallas_call(
        _oproj_router_kernel,
        grid=(n // tm,),
        in_specs=[pl.BlockSpec((tm, d), row), pl.BlockSpec((tm, d), row),
                  pl.BlockSpec((d, d), fixed), pl.BlockSpec((1, d), fixed),
                  pl.BlockSpec((d, LANES), fixed), pl.BlockSpec((d, LANES), fixed),
                  pl.BlockSpec((1, LANES), fixed), pl.BlockSpec((tm, tm), fixed)],
        out_specs=[pl.BlockSpec((tm, d), row), pl.BlockSpec((tm, d), row),
                   pl.BlockSpec((tm, LANES), row), pl.BlockSpec((8, LANES), fixed)],
        out_shape=[jax.ShapeDtypeStruct((n, d), f32), jax.ShapeDtypeStruct((n, d), f32),
                   jax.ShapeDtypeStruct((n, LANES), f32), jax.ShapeDtypeStruct((8, LANES), f32)],
        scratch_shapes=[pltpu.VMEM((1, LANES), f32)],
        compiler_params=_params("arbitrary"), name="oproj_router",
    )(o, h, wo, g.reshape(1, d), wr_hi, wr_lo, br, tri)


def _dispatch_kernel(zlo_ref, zhi_ref, pos_ref, x_ref, rows_ref, zero_ref, sem):
    tm = x_ref.shape[0]

    def row_copy(src, r):
        return pltpu.make_async_copy(src, rows_ref.at[pl.ds(r, 1)], sem)

    @pl.when(pl.program_id(0) == 0)
    def _():
        zero_ref[...] = jnp.zeros_like(zero_ref)
        for wait in (False, True):
            def per_expert(e, _, wait=wait):
                def per_row(r, _):
                    cp = row_copy(zero_ref, r)
                    cp.wait() if wait else cp.start()
                    return 0
                return lax.fori_loop(zlo_ref[e], zhi_ref[e], per_row, 0)
            lax.fori_loop(0, N_EXPERTS, per_expert, 0)

    for wait in (False, True):
        def per_token(t, _, wait=wait):
            for k in range(TOP_K):
                cp = row_copy(x_ref.at[pl.ds(t, 1)], pos_ref[t * TOP_K + k])
                cp.wait() if wait else cp.start(priority=k % 2)
            return 0
        lax.fori_loop(0, tm, per_token, 0, unroll=DMA_LOOP_UNROLL)


def dispatch(xn, pos_flat, zero_lo, zero_hi, n_rows):
    n, d = xn.shape
    tm = min(GATHER_BLOCK, n)
    grid_spec = pltpu.PrefetchScalarGridSpec(
        num_scalar_prefetch=2, grid=(n // tm,),
        in_specs=[pl.BlockSpec((tm * TOP_K,), lambda i, *_: (i,), memory_space=pltpu.SMEM),
                  pl.BlockSpec((tm, d), lambda i, *_: (i, 0))],
        out_specs=pl.BlockSpec(memory_space=pl.ANY),
        scratch_shapes=[pltpu.VMEM((1, d), f32), pltpu.SemaphoreType.DMA(())])
    return pl.pallas_call(
        _dispatch_kernel, grid_spec=grid_spec,
        out_shape=jax.ShapeDtypeStruct((n_rows, d), f32),
        compiler_params=_params("arbitrary"), name="dispatch",
    )(zero_lo, zero_hi, pos_flat, xn)


def _experts_kernel(be_ref, nused_ref, x_ref, wgu_ref, bgu_ref, wd_ref, bd_ref, y_ref, wgu_b, wd_b):
    rb = pl.program_id(0)
    d_ff = wd_ref.shape[2]
    used = rb < nused_ref[0]
    new_expert = (rb == 0) | (be_ref[rb] != be_ref[jnp.maximum(rb - 1, 0)])

    @pl.when(used & new_expert)
    def _():
        wgu_b[...] = wgu_ref[0, 0].astype(bf16)
        wd_b[...] = wd_ref[0, 0].astype(bf16)

    @pl.when(used)
    def _():
        x = x_ref[...].astype(bf16)
        hgu = jnp.dot(x, wgu_b[...], preferred_element_type=f32) + bgu_ref[0, 0]
        glu = jnp.minimum(hgu[:, :d_ff], SWIGLU_LIMIT)
        lin = jnp.clip(hgu[:, d_ff:], -SWIGLU_LIMIT, SWIGLU_LIMIT)
        act = glu * (1.0 / (1.0 + jnp.exp(-SWIGLU_ALPHA * glu))) * (lin + 1.0)
        y_ref[...] = jnp.dot(act.astype(bf16), wd_b[...], preferred_element_type=f32) + bd_ref[0, 0]

    @pl.when(jnp.logical_not(used))
    def _():
        y_ref[...] = jnp.zeros_like(y_ref)


def experts(x_rows, block_expert, n_used, layer, w_gu, b_gu, w_down, b_down):
    n_rows, d = x_rows.shape
    f2 = w_gu.shape[3]
    d_ff = f2 // 2
    wmap = lambda rb, be, nu: (layer, be[rb], 0, 0)
    rows = pl.BlockSpec((ROW_BLOCK, d), lambda rb, be, nu: (rb, 0))
    grid_spec = pltpu.PrefetchScalarGridSpec(
        num_scalar_prefetch=2, grid=(n_rows // ROW_BLOCK,),
        in_specs=[rows, pl.BlockSpec((1, 1, d, f2), wmap), pl.BlockSpec((1, 1, 1, f2), wmap),
                  pl.BlockSpec((1, 1, d_ff, d), wmap), pl.BlockSpec((1, 1, 1, d), wmap)],
        out_specs=rows,
        scratch_shapes=[pltpu.VMEM((d, f2), bf16), pltpu.VMEM((d_ff, d), bf16)])
    return pl.pallas_call(
        _experts_kernel, grid_spec=grid_spec,
        out_shape=jax.ShapeDtypeStruct((n_rows, d), f32),
        compiler_params=pltpu.CompilerParams(dimension_semantics=("arbitrary",),
                                             vmem_limit_bytes=EXPERTS_VMEM_LIMIT_BYTES),
        name="experts",
    )(block_expert, n_used, x_rows, w_gu, b_gu, w_down, b_down)


def _combine_ple_kernel(pos_ref, pos_next_ref, route_ref, h_ref, p_ref, g_ref, wg_ref, wp_ref, gf_ref,
                        yrows_ref, o_ref, ybuf0_ref, ybuf1_ref, sems, *, final):
    i = pl.program_id(0)
    tm = h_ref.shape[0]

    def row_copy(idx_ref, buf, s, t, k):
        r = 0 if idx_ref is None else idx_ref[t * TOP_K + k]
        return pltpu.make_async_copy(yrows_ref.at[pl.ds(r, 1)], buf.at[k, pl.ds(t, 1)], sems.at[s])

    def start_rows(idx_ref, buf, s, tokens):
        for t in tokens:
            for k in range(TOP_K):
                row_copy(idx_ref, buf, s, t, k).start(priority=k % 2)

    def wait_rows(buf, s):
        def per_token(t, _):
            for k in range(TOP_K):
                row_copy(None, buf, s, t, k).wait()
            return 0
        lax.fori_loop(0, tm, per_token, 0, unroll=DMA_LOOP_UNROLL)

    @pl.when(i == 0)
    def _():
        def per_token(t, _):
            start_rows(pos_ref, ybuf0_ref, 0, [t])
            return 0
        lax.fori_loop(0, tm, per_token, 0, unroll=DMA_LOOP_UNROLL)

    def step(cur, nxt, s_cur, s_nxt):
        wait_rows(cur, s_cur)
        start_rows(pos_next_ref, nxt, s_nxt, range(tm))
        proj = jnp.dot(p_ref[...].astype(bf16), wp_ref[...], preferred_element_type=f32)
        route = route_ref[...]
        h2 = h_ref[...]
        for k in range(TOP_K):
            h2 = h2 + route[:, ROUTE_GATE + k:ROUTE_GATE + k + 1] * cur[k]
        hn = _rmsnorm(h2, g_ref[...]).astype(bf16)
        gate = 1.0 / (1.0 + jnp.exp(-jnp.dot(hn, wg_ref[...], preferred_element_type=f32)))
        h3 = h2 + gate * proj
        if final:
            h3 = _rmsnorm(h3, gf_ref[...])
        o_ref[...] = h3

        @pl.when(i == pl.num_programs(0) - 1)
        def _():
            wait_rows(nxt, s_nxt)

    @pl.when(i % 2 == 0)
    def _():
        step(ybuf0_ref, ybuf1_ref, 0, 1)

    @pl.when(i % 2 == 1)
    def _():
        step(ybuf1_ref, ybuf0_ref, 1, 0)


def combine_ple(pos_flat, route, h1, y_rows, p, g, wg, wp, g_final, final):
    n, d = h1.shape
    pd = p.shape[1]
    tm = min(GATHER_BLOCK, n)
    last = n // tm - 1
    row = lambda i: (i, 0)
    fixed = lambda i: (0, 0)
    return pl.pallas_call(
        functools.partial(_combine_ple_kernel, final=final),
        grid=(n // tm,),
        in_specs=[pl.BlockSpec((tm * TOP_K,), lambda i: (i,), memory_space=pltpu.SMEM),
                  pl.BlockSpec((tm * TOP_K,), lambda i: (jnp.minimum(i + 1, last),), memory_space=pltpu.SMEM),
                  pl.BlockSpec((tm, LANES), row), pl.BlockSpec((tm, d), row),
                  pl.BlockSpec((tm, pd), row), pl.BlockSpec((1, d), fixed),
                  pl.BlockSpec((d, d), fixed), pl.BlockSpec((pd, d), fixed),
                  pl.BlockSpec((1, d), fixed), pl.BlockSpec(memory_space=pl.ANY)],
        out_specs=pl.BlockSpec((tm, d), row),
        out_shape=jax.ShapeDtypeStruct((n, d), f32),
        scratch_shapes=[pltpu.VMEM((TOP_K, tm, d), f32), pltpu.VMEM((TOP_K, tm, d), f32),
                        pltpu.SemaphoreType.DMA((2,))],
        compiler_params=_params("arbitrary"), name="combine_ple",
    )(pos_flat, pos_flat, route, h1, p, g.reshape(1, d), wg, wp, g_final.reshape(1, d), y_rows)


def _routing_tables(route, counts_f, n):
    counts = counts_f[0, :N_EXPERTS].astype(jnp.int32)
    padded = (counts + ROW_BLOCK - 1) // ROW_BLOCK * ROW_BLOCK
    pad_end = jnp.cumsum(padded)
    pad_start = pad_end - padded
    n_blocks = -(-(n * TOP_K + N_EXPERTS * (ROW_BLOCK - 1)) // ROW_BLOCK)
    n_rows = n_blocks * ROW_BLOCK
    idx = route[:, ROUTE_IDX:ROUTE_IDX + TOP_K].astype(jnp.int32)
    rank = route[:, ROUTE_RANK:ROUTE_RANK + TOP_K].astype(jnp.int32)
    pos_flat = (pad_start[idx] + rank).reshape(-1)
    block_start = jnp.arange(n_blocks, dtype=jnp.int32) * ROW_BLOCK
    block_expert = jnp.minimum(jnp.sum(block_start[:, None] >= pad_end[None, :], axis=1),
                               N_EXPERTS - 1).astype(jnp.int32)
    n_used = (pad_end[-1:] // ROW_BLOCK).astype(jnp.int32)
    zero_lo = (pad_start + counts).astype(jnp.int32)
    zero_hi = jnp.concatenate([pad_start[1:], jnp.array([n_rows], jnp.int32)]).astype(jnp.int32)
    return pos_flat, block_expert, n_used, zero_lo, zero_hi, n_rows


def kernel(x, p, g_mix, w_qkv, w_fgate, b_fgate, w_o, g_moe, w_router, b_router, w_gu, b_gu,
           w_down, b_down, g_ple, w_ple_gate, w_ple_proj, g_final):
    batch, seq, d = x.shape
    depth = w_qkv.shape[0]
    n = batch * seq
    h = x.reshape(n, d)
    q_scale = jnp.concatenate([jnp.full((d,), HEAD_DIM ** -0.5, f32), jnp.ones((2 * d,), f32)])
    n_exp = w_gu.shape[1]
    b_gu4 = b_gu.reshape(depth, n_exp, 1, -1)
    b_down4 = b_down.reshape(depth, n_exp, 1, -1)
    for i in range(depth):
        wqkv = (w_qkv[i] * q_scale).astype(bf16)
        if i % 2 == 0:
            qkv = norm_proj(h, g_mix[i], wqkv)
            o = attention(qkv, batch, seq)
        else:
            j = i // 2
            wf = jnp.pad(w_fgate[j], ((0, 0), (0, LANES - N_HEADS))).astype(bf16)
            bf = jnp.pad(b_fgate[j], (0, LANES - N_HEADS)).reshape(1, LANES)
            qkv, f_logit = norm_proj(h, g_mix[i], wqkv, wf)
            c = fox_cumsum(f_logit, bf, batch, seq)
            o = attention(qkv, batch, seq, c)

        wr = jnp.pad(w_router[i], ((0, 0), (0, LANES - N_EXPERTS)))
        wr_hi = wr.astype(bf16)
        wr_lo = (wr - wr_hi.astype(f32)).astype(bf16)
        br = jnp.pad(b_router[i], (0, LANES - N_EXPERTS)).reshape(1, LANES)
        h1, xn, route, counts = oproj_router(o, h, w_o[i].astype(bf16), g_moe[i], wr_hi, wr_lo, br)

        pos_flat, block_expert, n_used, zero_lo, zero_hi, n_rows = _routing_tables(route, counts, n)
        x_rows = dispatch(xn, pos_flat, zero_lo, zero_hi, n_rows)
        y_rows = experts(x_rows, block_expert, n_used, i, w_gu, b_gu4, w_down, b_down4)
        h = combine_ple(pos_flat, route, h1, y_rows, p[i].reshape(n, -1), g_ple[i],
                        w_ple_gate[i].astype(bf16), w_ple_proj[i].astype(bf16), g_final,
                        final=(i == depth - 1))
    return h.reshape(batch, seq, d)
```

```python
import functools

import jax
import jax.numpy as jnp
from jax import lax
from jax.experimental import pallas as pl
from jax.experimental.pallas import tpu as pltpu

N_HEADS = 16
HEAD_DIM = 64
N_EXPERTS = 32
TOP_K = 4
SWIGLU_LIMIT = 7.0
SWIGLU_ALPHA = 1.702
RMS_EPS = 1e-6

LANES = 128
SUBLANES = 8
HEADS_PER_STEP = LANES // HEAD_DIM
VMEM_LIMIT_BYTES = 48 * 1024 * 1024
EXPERTS_VMEM_LIMIT_BYTES = 56 * 1024 * 1024

ROW_BLOCK = 512
ATTN_BLOCK = 256
TOKEN_BLOCK = 512
GATHER_BLOCK = 256

SB_SKIP_BELOW = -104.0
DMA_LOOP_UNROLL = 8
ATTN_LANE_GROUPS = 2
FOX_KEY_BLOCK = 2048

ROUTE_IDX, ROUTE_GATE, ROUTE_RANK = 0, TOP_K, 2 * TOP_K

f32 = jnp.float32
bf16 = jnp.bfloat16


def _params(*semantics):
    return pltpu.CompilerParams(dimension_semantics=semantics, vmem_limit_bytes=VMEM_LIMIT_BYTES)


def _rmsnorm(x, g):
    var = jnp.mean(x * x, axis=-1, keepdims=True)
    return x * lax.rsqrt(var + RMS_EPS) * g


def _neg_abs(x):
    bits = lax.bitcast_convert_type(x, jnp.uint32) | jnp.uint32(0x80000000)
    return lax.bitcast_convert_type(bits, f32)


def _split2(x):
    hi = x.astype(bf16)
    lo = (x - hi.astype(f32)).astype(bf16)
    return hi, lo


def _norm_proj_kernel(h_ref, g_ref, w_ref, *rest, with_gate):
    xn = _rmsnorm(h_ref[...], g_ref[...]).astype(bf16)
    if with_gate:
        wf_ref, o_ref, of_ref = rest
        of_ref[...] = jnp.dot(xn, wf_ref[...], preferred_element_type=f32)
    else:
        (o_ref,) = rest
    o_ref[...] = jnp.dot(xn, w_ref[...], preferred_element_type=f32).astype(o_ref.dtype)


def norm_proj(h, g, w, wf=None):
    n, d = h.shape
    m = w.shape[1]
    tm = min(TOKEN_BLOCK, n)
    with_gate = wf is not None
    in_specs = [pl.BlockSpec((tm, d), lambda i: (i, 0)),
                pl.BlockSpec((1, d), lambda i: (0, 0)),
                pl.BlockSpec((d, m), lambda i: (0, 0))]
    out_specs = [pl.BlockSpec((tm, m), lambda i: (i, 0))]
    out_shape = [jax.ShapeDtypeStruct((n, m), bf16)]
    args = [h, g.reshape(1, d), w]
    if with_gate:
        in_specs.append(pl.BlockSpec((d, LANES), lambda i: (0, 0)))
        out_specs.append(pl.BlockSpec((tm, LANES), lambda i: (i, 0)))
        out_shape.append(jax.ShapeDtypeStruct((n, LANES), f32))
        args.append(wf)
    out = pl.pallas_call(
        functools.partial(_norm_proj_kernel, with_gate=with_gate),
        grid=(n // tm,), in_specs=in_specs, out_specs=out_specs, out_shape=out_shape,
        compiler_params=_params("parallel"), name="norm_proj")(*args)
    return out if with_gate else out[0]


def _fox_cumsum_kernel(f_ref, b_ref, tri_ref, c_ref, carry_ref):
    @pl.when(pl.program_id(1) == 0)
    def _():
        carry_ref[...] = jnp.zeros_like(carry_ref)

    x = f_ref[0] + b_ref[...]
    log_f = jnp.minimum(x, 0.0) - jnp.log(1.0 + jnp.exp(-jnp.abs(x)))
    p1 = log_f.astype(bf16)
    r1 = log_f - p1.astype(f32)
    p2 = r1.astype(bf16)
    p3 = (r1 - p2.astype(f32)).astype(bf16)
    tri = tri_ref[...]
    c = (jnp.dot(tri, p1, preferred_element_type=f32)
         + jnp.dot(tri, p2, preferred_element_type=f32)
         + jnp.dot(tri, p3, preferred_element_type=f32)) + carry_ref[...]
    c_ref[0] = c
    carry_ref[...] = c[-1:, :]


def fox_cumsum(f_logit, b_pad, batch, seq):
    ts = min(ATTN_BLOCK, seq)
    tri = (jnp.arange(ts)[None, :] <= jnp.arange(ts)[:, None]).astype(bf16)
    return pl.pallas_call(
        _fox_cumsum_kernel,
        grid=(batch, seq // ts),
        in_specs=[pl.BlockSpec((1, ts, LANES), lambda b, s: (b, s, 0)),
                  pl.BlockSpec((1, LANES), lambda b, s: (0, 0)),
                  pl.BlockSpec((ts, ts), lambda b, s: (0, 0))],
        out_specs=pl.BlockSpec((1, ts, LANES), lambda b, s: (b, s, 0)),
        out_shape=jax.ShapeDtypeStruct((batch, seq, LANES), f32),
        scratch_shapes=[pltpu.VMEM((1, LANES), f32)],
        compiler_params=_params("parallel", "arbitrary"), name="fox_cumsum",
    )(f_logit.reshape(batch, seq, LANES), b_pad, tri)


def _head_mask(hh):
    lane = lax.broadcasted_iota(jnp.int32, (1, LANES), 1)
    return (lane >= hh * HEAD_DIM) & (lane < (hh + 1) * HEAD_DIM)


def _qk(qm, k2):
    return lax.dot_general(qm, k2, (((1,), (1,)), ((), ())), preferred_element_type=f32)


def _sb_attn_kernel(q_ref, k_ref, v_ref, u_ref, o_ref):
    qi = pl.program_id(2)
    tq = q_ref.shape[1]
    tk = tq
    groups = q_ref.shape[2] // LANES
    u2 = u_ref[...]
    masks = [_head_mask(hh) for hh in range(HEADS_PER_STEP)]
    neg_qs = []
    for g in range(groups):
        q2 = q_ref[0, :, g * LANES:(g + 1) * LANES]
        neg_qs += [jnp.where(m, -q2, jnp.zeros_like(q2)) for m in masks]
    n_heads = len(neg_qs)

    def two_blocks(kb, runs, accs, diag):
        has_next = kb >= 1
        kbs = [kb, jnp.maximum(kb - 1, 0)]
        if diag:
            strict = (lax.broadcasted_iota(jnp.int32, (tq, tk), 1)
                      < lax.broadcasted_iota(jnp.int32, (tq, tk), 0))
        chains = [(blk, h) for blk in range(2) for h in range(n_heads)]

        def keys(ref, blk, h):
            g = h // HEADS_PER_STEP
            return ref[0, pl.ds(kbs[blk] * tk, tk), g * LANES:(g + 1) * LANES]

        nzs = {c: _qk(neg_qs[c[1]], keys(k_ref, *c)) for c in chains}
        suffixes = {}
        for c in chains:
            nz = nzs[c]
            log_not = jnp.minimum(nz, 0.0) - jnp.log(1.0 + jnp.exp(_neg_abs(nz)))
            if diag and c[0] == 0:
                log_not = jnp.where(strict, log_not, 0.0)
            hi, lo = _split2(log_not)
            suffixes[c] = jnp.dot(jnp.concatenate([hi, lo], axis=1), u2, preferred_element_type=f32)
        weights, new_runs = {}, []
        for h in range(n_heads):
            after = runs[h] + suffixes[(0, h)][:, 0:1]
            new_runs.append(jnp.where(has_next, after + suffixes[(1, h)][:, 0:1], after))
            for blk, run in ((0, runs[h]), (1, jnp.where(has_next, after, -jnp.inf))):
                a = jnp.exp((run - nzs[(blk, h)]) + suffixes[(blk, h)])
                if diag and blk == 0:
                    a = jnp.where(strict, a, 0.0)
                weights[(blk, h)] = a.astype(bf16)
        new_accs = []
        for g in range(groups):
            hs = range(g * HEADS_PER_STEP, (g + 1) * HEADS_PER_STEP)
            v2s = [keys(v_ref, blk, g * HEADS_PER_STEP) for blk in range(2)]
            vv = jnp.concatenate([jnp.where(m, v2, jnp.zeros_like(v2)) for v2 in v2s for m in masks], axis=0)
            ww = jnp.concatenate([weights[(blk, h)] for blk in range(2) for h in hs], axis=1)
            new_accs.append(accs[g] + jnp.dot(ww, vv, preferred_element_type=f32))
        return new_runs, new_accs

    def alive(runs):
        top = functools.reduce(jnp.maximum, runs)
        return (jnp.max(top) > SB_SKIP_BELOW).astype(jnp.int32)

    zero_run = jnp.zeros((tq, 1), f32)
    runs, accs = two_blocks(qi, [zero_run] * n_heads, [jnp.zeros((tq, LANES), f32)] * groups, True)

    def cond(c):
        return (c[0] >= 0) & (c[1] > 0)

    def body(c):
        runs, accs = two_blocks(c[0], list(c[2]), list(c[3]), False)
        return c[0] - 2, alive(runs), tuple(runs), tuple(accs)

    accs = lax.while_loop(cond, body, (qi - 2, alive(runs), tuple(runs), tuple(accs)))[3]
    o_ref[0] = jnp.concatenate(list(accs), axis=1).astype(o_ref.dtype)


def _fox_attn_kernel(q_ref, k_ref, v_ref, cq_ref, ck_ref, o_ref):
    hp = pl.program_id(1)
    qi = pl.program_id(2)
    tq = q_ref.shape[1]
    tk = min(FOX_KEY_BLOCK, k_ref.shape[1])
    groups = q_ref.shape[2] // LANES
    masks = [_head_mask(hh) for hh in range(HEADS_PER_STEP)]
    q0 = qi * tq
    n_full = q0 // tk
    lane = lax.broadcasted_iota(jnp.int32, (1, LANES), 1)
    c_blk = cq_ref[0]
    qms, cq_ps = [], []
    for g in range(groups):
        q2 = q_ref[0, :, g * LANES:(g + 1) * LANES]
        qms.append([jnp.where(m, q2, jnp.zeros_like(q2)) for m in masks])
        head0 = (hp * groups + g) * HEADS_PER_STEP
        cqs = [jnp.sum(jnp.where(lane == head0 + hh, c_blk, 0.0), axis=1, keepdims=True)
               for hh in range(HEADS_PER_STEP)]
        cq_ps.append(jnp.where(masks[0], cqs[0], cqs[1]))

    def step(k_start, width, masked, carry):
        k_start = pl.multiple_of(k_start, LANES)
        if masked:
            own = (lax.broadcasted_iota(jnp.int32, (tq, tq), 1) <= lax.broadcasted_iota(jnp.int32, (tq, tq), 0))

        def causal(w):
            own_w = jnp.where(own, w[:, width - tq:], -jnp.inf)
            return own_w if width == tq else jnp.concatenate([w[:, :width - tq], own_w], axis=1)
        ws = []
        for g in range(groups):
            k2 = k_ref[0, pl.ds(k_start, width), g * LANES:(g + 1) * LANES]
            for hh in range(HEADS_PER_STEP):
                head = (hp * groups + g) * HEADS_PER_STEP + hh
                ck = ck_ref[0, pl.ds(head, 1), pl.ds(k_start, width)]
                w = _qk(qms[g][hh], k2) - ck
                ws.append(causal(w) if masked else w)
        out = []
        for g in range(groups):
            m_p, l_p, acc = carry[g]
            w0, w1 = ws[g * HEADS_PER_STEP], ws[g * HEADS_PER_STEP + 1]
            mx_p = jnp.where(masks[0], jnp.max(w0, axis=1, keepdims=True), jnp.max(w1, axis=1, keepdims=True))
            m_new = jnp.maximum(m_p, cq_ps[g] + mx_p)
            shift = m_new - cq_ps[g]
            alpha = jnp.exp(m_p - m_new)
            p0 = jnp.exp(w0 - shift[:, 0:1])
            p1 = jnp.exp(w1 - shift[:, HEAD_DIM:HEAD_DIM + 1])
            sum_p = jnp.where(masks[0], jnp.sum(p0, axis=1, keepdims=True), jnp.sum(p1, axis=1, keepdims=True))
            v2 = v_ref[0, pl.ds(k_start, width), g * LANES:(g + 1) * LANES]
            vv = jnp.concatenate([jnp.where(m, v2, jnp.zeros_like(v2)) for m in masks], axis=0)
            acc = acc * alpha + jnp.dot(jnp.concatenate([p0.astype(bf16), p1.astype(bf16)], axis=1), vv,
                                        preferred_element_type=f32)
            out.append((m_new, alpha * l_p + sum_p, acc))
        return tuple(out)

    init = tuple((jnp.full((tq, LANES), -jnp.inf, f32), jnp.zeros((tq, LANES), f32),
                  jnp.zeros((tq, LANES), f32)) for _ in range(groups))
    c = init if tk == k_ref.shape[1] else lax.fori_loop(0, n_full, lambda kb, c: step(kb * tk, tk, False, c), init)
    variants = [functools.partial(step, n_full * tk, (j + 1) * tq, True) for j in range(tk // tq)]
    c = variants[0](c) if len(variants) == 1 else lax.switch((q0 - n_full * tk) // tq, variants, c)
    o_ref[0] = jnp.concatenate([acc / l_p for _, l_p, acc in c], axis=1).astype(o_ref.dtype)


def attention(qkv, batch, seq, c=None):
    d = qkv.shape[1] // 3
    width = LANES * ATTN_LANE_GROUPS
    nlb = d // width
    tq = min(ATTN_BLOCK, seq)
    qkv3 = qkv.reshape(batch, seq, 3 * d)
    in_specs = [pl.BlockSpec((1, tq, width), lambda b, hp, qi: (b, qi, hp)),
                pl.BlockSpec((1, seq, width), lambda b, hp, qi: (b, 0, nlb + hp)),
                pl.BlockSpec((1, seq, width), lambda b, hp, qi: (b, 0, 2 * nlb + hp))]
    args = [qkv3, qkv3, qkv3]
    if c is None:
        kern = _sb_attn_kernel
        u = (jnp.arange(tq)[:, None] >= jnp.arange(tq)[None, :]).astype(bf16)
        in_specs.append(pl.BlockSpec((2 * tq, tq), lambda b, hp, qi: (0, 0)))
        args.append(jnp.concatenate([u, u], axis=0))
    else:
        kern = _fox_attn_kernel
        tk = min(FOX_KEY_BLOCK, seq)
        assert tk % tq == 0 and seq % tk == 0
        ch = jnp.transpose(c[:, :, :N_HEADS], (0, 2, 1))
        in_specs.append(pl.BlockSpec((1, tq, LANES), lambda b, hp, qi: (b, qi, 0)))
        in_specs.append(pl.BlockSpec((1, N_HEADS, seq), lambda b, hp, qi: (b, 0, 0)))
        args += [c, ch]
    out = pl.pallas_call(
        kern, grid=(batch, nlb, seq // tq), in_specs=in_specs,
        out_specs=pl.BlockSpec((1, tq, width), lambda b, hp, qi: (b, qi, hp)),
        out_shape=jax.ShapeDtypeStruct((batch, seq, d), bf16),
        compiler_params=_params("parallel", "parallel", "arbitrary"), name="attention")(*args)
    return out.reshape(batch * seq, d)


def _oproj_router_kernel(o_ref, h_ref, wo_ref, g_ref, wrh_ref, wrl_ref, br_ref, tri_ref,
                         h1_ref, xn_ref, route_ref, cnt_ref, carry_ref):
    @pl.when(pl.program_id(0) == 0)
    def _():
        carry_ref[...] = jnp.zeros_like(carry_ref)

    tm = h_ref.shape[0]
    h1 = h_ref[...] + jnp.dot(o_ref[...], wo_ref[...], preferred_element_type=f32)
    h1_ref[...] = h1
    xn = _rmsnorm(h1, g_ref[...])
    xn_ref[...] = xn

    xh, xl = _split2(xn)
    logits = (jnp.dot(xh, wrh_ref[...], preferred_element_type=f32)
              + jnp.dot(xh, wrl_ref[...], preferred_element_type=f32)
              + jnp.dot(xl, wrh_ref[...], preferred_element_type=f32)) + br_ref[...]
    lane = lax.broadcasted_iota(jnp.int32, (tm, LANES), 1)
    lane_f = lane.astype(f32)
    work = jnp.where(lane < N_EXPERTS, logits, -jnp.inf)

    vals, idxs, hots = [], [], []
    for _ in range(TOP_K):
        mx = jnp.max(work, axis=1, keepdims=True)
        idx = jnp.min(jnp.where(work == mx, lane_f, float(LANES)), axis=1, keepdims=True)
        hot = lane_f == idx
        work = jnp.where(hot, -jnp.inf, work)
        vals.append(mx)
        idxs.append(idx)
        hots.append(hot)

    exps = [jnp.exp(v - vals[0]) for v in vals]
    denom = exps[0] + exps[1] + exps[2] + exps[3]
    gates = [e / denom for e in exps]

    sel = jnp.zeros((tm, LANES), f32)
    for hot in hots:
        sel = sel + hot.astype(f32)
    before = jnp.dot(tri_ref[...], sel.astype(bf16), preferred_element_type=f32) + carry_ref[...]
    ranks = [jnp.sum(jnp.where(hot, before, 0.0), axis=1, keepdims=True) for hot in hots]
    carry = carry_ref[...] + jnp.sum(sel, axis=0, keepdims=True)
    carry_ref[...] = carry
    cnt_ref[...] = jnp.broadcast_to(carry, cnt_ref.shape)

    route = jnp.zeros((tm, LANES), f32)
    for base, cols in ((ROUTE_IDX, idxs), (ROUTE_GATE, gates), (ROUTE_RANK, ranks)):
        for k, colv in enumerate(cols):
            route = jnp.where(lane == base + k, colv, route)
    route_ref[...] = route


def oproj_router(o, h, wo, g, wr_hi, wr_lo, br):
    n, d = h.shape
    tm = min(TOKEN_BLOCK, n)
    tri = (jnp.arange(tm)[None, :] < jnp.arange(tm)[:, None]).astype(bf16)
    row = lambda i: (i, 0)
    fixed = lambda i: (0, 0)
    return pl.pallas_call(
        _oproj_router_kernel,
        grid=(n // tm,),
        in_specs=[pl.BlockSpec((tm, d), row), pl.BlockSpec((tm, d), row),
                  pl.BlockSpec((d, d), fixed), pl.BlockSpec((1, d), fixed),
                  pl.BlockSpec((d, LANES), fixed), pl.BlockSpec((d, LANES), fixed),
                  pl.BlockSpec((1, LANES), fixed), pl.BlockSpec((tm, tm), fixed)],
        out_specs=[pl.BlockSpec((tm, d), row), pl.BlockSpec((tm, d), row),
                   pl.BlockSpec((tm, LANES), row), pl.BlockSpec((8, LANES), fixed)],
        out_shape=[jax.ShapeDtypeStruct((n, d), f32), jax.ShapeDtypeStruct((n, d), f32),
                   jax.ShapeDtypeStruct((n, LANES), f32), jax.ShapeDtypeStruct((8, LANES), f32)],
        scratch_shapes=[pltpu.VMEM((1, LANES), f32)],
        compiler_params=_params("arbitrary"), name="oproj_router",
    )(o, h, wo, g.reshape(1, d), wr_hi, wr_lo, br, tri)


def _dispatch_kernel(zlo_ref, zhi_ref, pos_ref, x_ref, rows_ref, zero_ref, sem):
    tm = x_ref.shape[0]

    def row_copy(src, r):
        return pltpu.make_async_copy(src, rows_ref.at[pl.ds(r, 1)], sem)

    @pl.when(pl.program_id(0) == 0)
    def _():
        zero_ref[...] = jnp.zeros_like(zero_ref)
        zrows = zero_ref.shape[0]

        def chunk(start, size, wait):
            if size > 1:
                start = pl.multiple_of(start, SUBLANES)
            cp = pltpu.make_async_copy(zero_ref.at[pl.ds(0, size)], rows_ref.at[pl.ds(start, size)], sem)
            cp.wait() if wait else cp.start()

        for wait in (False, True):
            def per_expert(e, _, wait=wait):
                hi = zhi_ref[e]
                aligned = jnp.minimum(hi, (zlo_ref[e] + SUBLANES - 1) // SUBLANES * SUBLANES)

                def single(r, _):
                    chunk(r, 1, wait)
                    return 0
                lax.fori_loop(zlo_ref[e], aligned, single, 0)
                length = hi - aligned
                n_whole = length // zrows

                def whole(j, _):
                    chunk(aligned + j * zrows, zrows, wait)
                    return 0
                lax.fori_loop(0, n_whole, whole, 0)
                start = aligned + n_whole * zrows
                size = zrows // 2
                while size >= SUBLANES:
                    has = (length & size) != 0

                    @pl.when(has)
                    def _(start=start, size=size):
                        chunk(start, size, wait)
                    start = start + jnp.where(has, size, 0)
                    size //= 2
                return 0
            lax.fori_loop(0, N_EXPERTS, per_expert, 0)

    for wait in (False, True):
        def per_token(t, _, wait=wait):
            for k in range(TOP_K):
                cp = row_copy(x_ref.at[pl.ds(t, 1)], pos_ref[t * TOP_K + k])
                cp.wait() if wait else cp.start(priority=k % 2)
            return 0
        lax.fori_loop(0, tm, per_token, 0, unroll=DMA_LOOP_UNROLL)


def dispatch(xn, pos_flat, zero_lo, zero_hi, n_rows):
    n, d = xn.shape
    tm = min(GATHER_BLOCK, n)
    grid_spec = pltpu.PrefetchScalarGridSpec(
        num_scalar_prefetch=2, grid=(n // tm,),
        in_specs=[pl.BlockSpec((tm * TOP_K,), lambda i, *_: (i,), memory_space=pltpu.SMEM),
                  pl.BlockSpec((tm, d), lambda i, *_: (i, 0))],
        out_specs=pl.BlockSpec(memory_space=pl.ANY),
        scratch_shapes=[pltpu.VMEM((ROW_BLOCK, d), f32), pltpu.SemaphoreType.DMA(())])
    return pl.pallas_call(
        _dispatch_kernel, grid_spec=grid_spec,
        out_shape=jax.ShapeDtypeStruct((n_rows, d), f32),
        compiler_params=_params("arbitrary"), name="dispatch",
    )(zero_lo, zero_hi, pos_flat, xn)


def _experts_kernel(be_ref, nused_ref, x_ref, wgu_ref, bgu_ref, wd_ref, bd_ref, y_ref, wgu_b, wd_b):
    rb = pl.program_id(0)
    d_ff = wd_ref.shape[2]
    used = rb < nused_ref[0]
    new_expert = (rb == 0) | (be_ref[rb] != be_ref[jnp.maximum(rb - 1, 0)])

    @pl.when(used & new_expert)
    def _():
        wgu_b[...] = wgu_ref[0, 0].astype(bf16)
        wd_b[...] = wd_ref[0, 0].astype(bf16)

    @pl.when(used)
    def _():
        x = x_ref[...].astype(bf16)
        hgu = jnp.dot(x, wgu_b[...], preferred_element_type=f32) + bgu_ref[0, 0]
        glu = jnp.minimum(hgu[:, :d_ff], SWIGLU_LIMIT)
        lin = jnp.clip(hgu[:, d_ff:], -SWIGLU_LIMIT, SWIGLU_LIMIT)
        act = glu * (1.0 / (1.0 + jnp.exp(-SWIGLU_ALPHA * glu))) * (lin + 1.0)
        y_ref[...] = jnp.dot(act.astype(bf16), wd_b[...], preferred_element_type=f32) + bd_ref[0, 0]

    @pl.when(jnp.logical_not(used))
    def _():
        y_ref[...] = jnp.zeros_like(y_ref)


def experts(x_rows, block_expert, n_used, layer, w_gu, b_gu, w_down, b_down):
    n_rows, d = x_rows.shape
    f2 = w_gu.shape[3]
    d_ff = f2 // 2
    wmap = lambda rb, be, nu: (layer, be[rb], 0, 0)
    rows = pl.BlockSpec((ROW_BLOCK, d), lambda rb, be, nu: (rb, 0))
    grid_spec = pltpu.PrefetchScalarGridSpec(
        num_scalar_prefetch=2, grid=(n_rows // ROW_BLOCK,),
        in_specs=[rows, pl.BlockSpec((1, 1, d, f2), wmap), pl.BlockSpec((1, 1, 1, f2), wmap),
                  pl.BlockSpec((1, 1, d_ff, d), wmap), pl.BlockSpec((1, 1, 1, d), wmap)],
        out_specs=rows,
        scratch_shapes=[pltpu.VMEM((d, f2), bf16), pltpu.VMEM((d_ff, d), bf16)])
    return pl.pallas_call(
        _experts_kernel, grid_spec=grid_spec,
        out_shape=jax.ShapeDtypeStruct((n_rows, d), f32),
        compiler_params=pltpu.CompilerParams(dimension_semantics=("arbitrary",),
                                             vmem_limit_bytes=EXPERTS_VMEM_LIMIT_BYTES),
        name="experts",
    )(block_expert, n_used, x_rows, w_gu, b_gu, w_down, b_down)


def _combine_ple_kernel(pos_ref, pos_next_ref, route_ref, h_ref, p_ref, g_ref, wg_ref, wp_ref, gf_ref,
                        yrows_ref, o_ref, ybuf0_ref, ybuf1_ref, sems, *, final):
    i = pl.program_id(0)
    tm = h_ref.shape[0]

    def row_copy(idx_ref, buf, s, t, k):
        r = 0 if idx_ref is None else idx_ref[t * TOP_K + k]
        return pltpu.make_async_copy(yrows_ref.at[pl.ds(r, 1)], buf.at[k, pl.ds(t, 1)], sems.at[s])

    def start_rows(idx_ref, buf, s, tokens):
        for t in tokens:
            for k in range(TOP_K):
                row_copy(idx_ref, buf, s, t, k).start(priority=k % 2)

    def wait_rows(buf, s):
        def per_token(t, _):
            for k in range(TOP_K):
                row_copy(None, buf, s, t, k).wait()
            return 0
        lax.fori_loop(0, tm, per_token, 0, unroll=DMA_LOOP_UNROLL)

    @pl.when(i == 0)
    def _():
        def per_token(t, _):
            start_rows(pos_ref, ybuf0_ref, 0, [t])
            return 0
        lax.fori_loop(0, tm, per_token, 0, unroll=DMA_LOOP_UNROLL)

    def step(cur, nxt, s_cur, s_nxt):
        wait_rows(cur, s_cur)
        start_rows(pos_next_ref, nxt, s_nxt, range(tm))
        proj = jnp.dot(p_ref[...].astype(bf16), wp_ref[...], preferred_element_type=f32)
        route = route_ref[...]
        h2 = h_ref[...]
        for k in range(TOP_K):
            h2 = h2 + route[:, ROUTE_GATE + k:ROUTE_GATE + k + 1] * cur[k]
        hn = _rmsnorm(h2, g_ref[...]).astype(bf16)
        gate = 1.0 / (1.0 + jnp.exp(-jnp.dot(hn, wg_ref[...], preferred_element_type=f32)))
        h3 = h2 + gate * proj
        if final:
            h3 = _rmsnorm(h3, gf_ref[...])
        o_ref[...] = h3

        @pl.when(i == pl.num_programs(0) - 1)
        def _():
            wait_rows(nxt, s_nxt)

    @pl.when(i % 2 == 0)
    def _():
        step(ybuf0_ref, ybuf1_ref, 0, 1)

    @pl.when(i % 2 == 1)
    def _():
        step(ybuf1_ref, ybuf0_ref, 1, 0)


def combine_ple(pos_flat, route, h1, y_rows, p, g, wg, wp, g_final, final):
    n, d = h1.shape
    pd = p.shape[1]
    tm = min(GATHER_BLOCK, n)
    last = n // tm - 1
    row = lambda i: (i, 0)
    fixed = lambda i: (0, 0)
    return pl.pallas_call(
        functools.partial(_combine_ple_kernel, final=final),
        grid=(n // tm,),
        in_specs=[pl.BlockSpec((tm * TOP_K,), lambda i: (i,), memory_space=pltpu.SMEM),
                  pl.BlockSpec((tm * TOP_K,), lambda i: (jnp.minimum(i + 1, last),), memory_space=pltpu.SMEM),
                  pl.BlockSpec((tm, LANES), row), pl.BlockSpec((tm, d), row),
                  pl.BlockSpec((tm, pd), row), pl.BlockSpec((1, d), fixed),
                  pl.BlockSpec((d, d), fixed), pl.BlockSpec((pd, d), fixed),
                  pl.BlockSpec((1, d), fixed), pl.BlockSpec(memory_space=pl.ANY)],
        out_specs=pl.BlockSpec((tm, d), row),
        out_shape=jax.ShapeDtypeStruct((n, d), f32),
        scratch_shapes=[pltpu.VMEM((TOP_K, tm, d), f32), pltpu.VMEM((TOP_K, tm, d), f32),
                        pltpu.SemaphoreType.DMA((2,))],
        compiler_params=_params("arbitrary"), name="combine_ple",
    )(pos_flat, pos_flat, route, h1, p, g.reshape(1, d), wg, wp, g_final.reshape(1, d), y_rows)


def _routing_tables(route, counts_f, n):
    counts = counts_f[0, :N_EXPERTS].astype(jnp.int32)
    padded = (counts + ROW_BLOCK - 1) // ROW_BLOCK * ROW_BLOCK
    pad_end = jnp.cumsum(padded)
    pad_start = pad_end - padded
    n_blocks = -(-(n * TOP_K + N_EXPERTS * (ROW_BLOCK - 1)) // ROW_BLOCK)
    n_rows = n_blocks * ROW_BLOCK
    idx = route[:, ROUTE_IDX:ROUTE_IDX + TOP_K].astype(jnp.int32)
    rank = route[:, ROUTE_RANK:ROUTE_RANK + TOP_K].astype(jnp.int32)
    pos_flat = (pad_start[idx] + rank).reshape(-1)
    block_start = jnp.arange(n_blocks, dtype=jnp.int32) * ROW_BLOCK
    block_expert = jnp.minimum(jnp.sum(block_start[:, None] >= pad_end[None, :], axis=1),
                               N_EXPERTS - 1).astype(jnp.int32)
    n_used = (pad_end[-1:] // ROW_BLOCK).astype(jnp.int32)
    zero_lo = (pad_start + counts).astype(jnp.int32)
    zero_hi = jnp.concatenate([pad_start[1:], jnp.array([n_rows], jnp.int32)]).astype(jnp.int32)
    return pos_flat, block_expert, n_used, zero_lo, zero_hi, n_rows


def kernel(x, p, g_mix, w_qkv, w_fgate, b_fgate, w_o, g_moe, w_router, b_router, w_gu, b_gu,
           w_down, b_down, g_ple, w_ple_gate, w_ple_proj, g_final):
    batch, seq, d = x.shape
    depth = w_qkv.shape[0]
    n = batch * seq
    h = x.reshape(n, d)
    q_scale = jnp.concatenate([jnp.full((d,), HEAD_DIM ** -0.5, f32), jnp.ones((2 * d,), f32)])
    n_exp = w_gu.shape[1]
    b_gu4 = b_gu.reshape(depth, n_exp, 1, -1)
    b_down4 = b_down.reshape(depth, n_exp, 1, -1)
    for i in range(depth):
        wqkv = (w_qkv[i] * q_scale).astype(bf16)
        if i % 2 == 0:
            qkv = norm_proj(h, g_mix[i], wqkv)
            o = attention(qkv, batch, seq)
        else:
            j = i // 2
            wf = jnp.pad(w_fgate[j], ((0, 0), (0, LANES - N_HEADS))).astype(bf16)
            bf = jnp.pad(b_fgate[j], (0, LANES - N_HEADS)).reshape(1, LANES)
            qkv, f_logit = norm_proj(h, g_mix[i], wqkv, wf)
            c = fox_cumsum(f_logit, bf, batch, seq)
            o = attention(qkv, batch, seq, c)

        wr = jnp.pad(w_router[i], ((0, 0), (0, LANES - N_EXPERTS)))
        wr_hi = wr.astype(bf16)
        wr_lo = (wr - wr_hi.astype(f32)).astype(bf16)
        br = jnp.pad(b_router[i], (0, LANES - N_EXPERTS)).reshape(1, LANES)
        h1, xn, route, counts = oproj_router(o, h, w_o[i].astype(bf16), g_moe[i], wr_hi, wr_lo, br)

        pos_flat, block_expert, n_used, zero_lo, zero_hi, n_rows = _routing_tables(route, counts, n)
        x_rows = dispatch(xn, pos_flat, zero_lo, zero_hi, n_rows)
        y_rows = experts(x_rows, block_expert, n_used, i, w_gu, b_gu4, w_down, b_down4)
        h = combine_ple(pos_flat, route, h1, y_rows, p[i].reshape(n, -1), g_ple[i],
                        w_ple_gate[i].astype(bf16), w_ple_proj[i].astype(bf16), g_final,
                        final=(i == depth - 1))
    return h.reshape(batch, seq, d)
```

```python
import functools

import jax
import jax.numpy as jnp
from jax import lax
from jax.experimental import pallas as pl
from jax.experimental.pallas import tpu as pltpu

N_HEADS = 16
HEAD_DIM = 64
N_EXPERTS = 32
TOP_K = 4
SWIGLU_LIMIT = 7.0
SWIGLU_ALPHA = 1.702
RMS_EPS = 1e-6

LANES = 128
SUBLANES = 8
HEADS_PER_STEP = LANES // HEAD_DIM
VMEM_LIMIT_BYTES = 48 * 1024 * 1024
EXPERTS_VMEM_LIMIT_BYTES = 56 * 1024 * 1024

ROW_BLOCK = 512
ATTN_BLOCK = 256
TOKEN_BLOCK = 512
GATHER_BLOCK = 256
SCATTER_BLOCK = 512

SB_SKIP_BELOW = -104.0
DMA_LOOP_UNROLL = 8
ATTN_LANE_GROUPS = 2
FOX_KEY_BLOCK = 2048

ROUTE_IDX, ROUTE_GATE, ROUTE_RANK = 0, TOP_K, 2 * TOP_K

f32 = jnp.float32
bf16 = jnp.bfloat16


def _params(*semantics):
    return pltpu.CompilerParams(dimension_semantics=semantics, vmem_limit_bytes=VMEM_LIMIT_BYTES)


def _rmsnorm(x, g):
    var = jnp.mean(x * x, axis=-1, keepdims=True)
    return x * lax.rsqrt(var + RMS_EPS) * g


def _neg_abs(x):
    bits = lax.bitcast_convert_type(x, jnp.uint32) | jnp.uint32(0x80000000)
    return lax.bitcast_convert_type(bits, f32)


def _split2(x):
    hi = x.astype(bf16)
    lo = (x - hi.astype(f32)).astype(bf16)
    return hi, lo


def _norm_proj_kernel(h_ref, g_ref, w_ref, *rest, with_gate):
    xn = _rmsnorm(h_ref[...], g_ref[...]).astype(bf16)
    if with_gate:
        wf_ref, o_ref, of_ref = rest
        of_ref[...] = jnp.dot(xn, wf_ref[...], preferred_element_type=f32)
    else:
        (o_ref,) = rest
    o_ref[...] = jnp.dot(xn, w_ref[...], preferred_element_type=f32).astype(o_ref.dtype)


def norm_proj(h, g, w, wf=None):
    n, d = h.shape
    m = w.shape[1]
    tm = min(TOKEN_BLOCK, n)
    with_gate = wf is not None
    in_specs = [pl.BlockSpec((tm, d), lambda i: (i, 0)),
                pl.BlockSpec((1, d), lambda i: (0, 0)),
                pl.BlockSpec((d, m), lambda i: (0, 0))]
    out_specs = [pl.BlockSpec((tm, m), lambda i: (i, 0))]
    out_shape = [jax.ShapeDtypeStruct((n, m), bf16)]
    args = [h, g.reshape(1, d), w]
    if with_gate:
        in_specs.append(pl.BlockSpec((d, LANES), lambda i: (0, 0)))
        out_specs.append(pl.BlockSpec((tm, LANES), lambda i: (i, 0)))
        out_shape.append(jax.ShapeDtypeStruct((n, LANES), f32))
        args.append(wf)
    out = pl.pallas_call(
        functools.partial(_norm_proj_kernel, with_gate=with_gate),
        grid=(n // tm,), in_specs=in_specs, out_specs=out_specs, out_shape=out_shape,
        compiler_params=_params("parallel"), name="norm_proj")(*args)
    return out if with_gate else out[0]


def _fox_cumsum_kernel(f_ref, b_ref, tri_ref, c_ref, carry_ref):
    @pl.when(pl.program_id(1) == 0)
    def _():
        carry_ref[...] = jnp.zeros_like(carry_ref)

    x = f_ref[0] + b_ref[...]
    log_f = jnp.minimum(x, 0.0) - jnp.log(1.0 + jnp.exp(-jnp.abs(x)))
    p1 = log_f.astype(bf16)
    r1 = log_f - p1.astype(f32)
    p2 = r1.astype(bf16)
    p3 = (r1 - p2.astype(f32)).astype(bf16)
    tri = tri_ref[...]
    c = (jnp.dot(tri, p1, preferred_element_type=f32)
         + jnp.dot(tri, p2, preferred_element_type=f32)
         + jnp.dot(tri, p3, preferred_element_type=f32)) + carry_ref[...]
    c_ref[0] = c
    carry_ref[...] = c[-1:, :]


def fox_cumsum(f_logit, b_pad, batch, seq):
    ts = min(ATTN_BLOCK, seq)
    tri = (jnp.arange(ts)[None, :] <= jnp.arange(ts)[:, None]).astype(bf16)
    return pl.pallas_call(
        _fox_cumsum_kernel,
        grid=(batch, seq // ts),
        in_specs=[pl.BlockSpec((1, ts, LANES), lambda b, s: (b, s, 0)),
                  pl.BlockSpec((1, LANES), lambda b, s: (0, 0)),
                  pl.BlockSpec((ts, ts), lambda b, s: (0, 0))],
        out_specs=pl.BlockSpec((1, ts, LANES), lambda b, s: (b, s, 0)),
        out_shape=jax.ShapeDtypeStruct((batch, seq, LANES), f32),
        scratch_shapes=[pltpu.VMEM((1, LANES), f32)],
        compiler_params=_params("parallel", "arbitrary"), name="fox_cumsum",
    )(f_logit.reshape(batch, seq, LANES), b_pad, tri)


def _head_mask(hh):
    lane = lax.broadcasted_iota(jnp.int32, (1, LANES), 1)
    return (lane >= hh * HEAD_DIM) & (lane < (hh + 1) * HEAD_DIM)


def _qk(qm, k2):
    return lax.dot_general(qm, k2, (((1,), (1,)), ((), ())), preferred_element_type=f32)


def _sb_attn_kernel(q_ref, k_ref, v_ref, u_ref, o_ref):
    qi = pl.program_id(2)
    tq = q_ref.shape[1]
    tk = tq
    groups = q_ref.shape[2] // LANES
    u2 = u_ref[...]
    masks = [_head_mask(hh) for hh in range(HEADS_PER_STEP)]
    neg_qs = []
    for g in range(groups):
        q2 = q_ref[0, :, g * LANES:(g + 1) * LANES]
        neg_qs += [jnp.where(m, -q2, jnp.zeros_like(q2)) for m in masks]
    n_heads = len(neg_qs)

    def blocks(kb, runs, accs, diag, n_blk=2):
        has_next = kb >= 1
        kbs = [kb, jnp.maximum(kb - 1, 0)]
        if diag:
            strict = (lax.broadcasted_iota(jnp.int32, (tq, tk), 1)
                      < lax.broadcasted_iota(jnp.int32, (tq, tk), 0))
        chains = [(blk, h) for blk in range(n_blk) for h in range(n_heads)]

        def keys(ref, blk, h):
            g = h // HEADS_PER_STEP
            return ref[0, pl.ds(kbs[blk] * tk, tk), g * LANES:(g + 1) * LANES]

        nzs = {c: _qk(neg_qs[c[1]], keys(k_ref, *c)) for c in chains}
        suffixes = {}
        for c in chains:
            nz = nzs[c]
            log_not = jnp.minimum(nz, 0.0) - jnp.log(1.0 + jnp.exp(_neg_abs(nz)))
            if diag and c[0] == 0:
                log_not = jnp.where(strict, log_not, 0.0)
            hi, lo = _split2(log_not)
            suffixes[c] = jnp.dot(jnp.concatenate([hi, lo], axis=1), u2, preferred_element_type=f32)
        weights, new_runs = {}, []
        for h in range(n_heads):
            after = runs[h] + suffixes[(0, h)][:, 0:1]
            if n_blk == 1:
                new_runs.append(after)
            else:
                new_runs.append(jnp.where(has_next, after + suffixes[(1, h)][:, 0:1], after))
            for blk, run in ((0, runs[h]), (1, jnp.where(has_next, after, -jnp.inf)))[:n_blk]:
                a = jnp.exp((run - nzs[(blk, h)]) + suffixes[(blk, h)])
                if diag and blk == 0:
                    a = jnp.where(strict, a, 0.0)
                weights[(blk, h)] = a.astype(bf16)
        new_accs = []
        for g in range(groups):
            hs = range(g * HEADS_PER_STEP, (g + 1) * HEADS_PER_STEP)
            v2s = [keys(v_ref, blk, g * HEADS_PER_STEP) for blk in range(n_blk)]
            vv = jnp.concatenate([jnp.where(m, v2, jnp.zeros_like(v2)) for v2 in v2s for m in masks], axis=0)
            ww = jnp.concatenate([weights[(blk, h)] for blk in range(n_blk) for h in hs], axis=1)
            new_accs.append(accs[g] + jnp.dot(ww, vv, preferred_element_type=f32))
        return new_runs, new_accs

    def alive(runs):
        top = functools.reduce(jnp.maximum, runs)
        return (jnp.max(top) > SB_SKIP_BELOW).astype(jnp.int32)

    zero_run = jnp.zeros((tq, 1), f32)
    start = (tuple([zero_run] * n_heads), tuple([jnp.zeros((tq, LANES), f32)] * groups))
    runs, accs = lax.cond(qi == 0,
                          lambda c: tuple(map(tuple, blocks(qi, list(c[0]), list(c[1]), True, 1))),
                          lambda c: tuple(map(tuple, blocks(qi, list(c[0]), list(c[1]), True))), start)

    def cond(c):
        return (c[0] >= 0) & (c[1] > 0)

    def body(c):
        runs, accs = blocks(c[0], list(c[2]), list(c[3]), False)
        return c[0] - 2, alive(runs), tuple(runs), tuple(accs)

    accs = lax.while_loop(cond, body, (qi - 2, alive(runs), tuple(runs), tuple(accs)))[3]
    o_ref[0] = jnp.concatenate(list(accs), axis=1).astype(o_ref.dtype)


def _fox_attn_kernel(q_ref, k_ref, v_ref, cq_ref, ck_ref, o_ref):
    hp = pl.program_id(1)
    qi = pl.program_id(2)
    tq = q_ref.shape[1]
    tk = min(FOX_KEY_BLOCK, k_ref.shape[1])
    groups = q_ref.shape[2] // LANES
    masks = [_head_mask(hh) for hh in range(HEADS_PER_STEP)]
    q0 = qi * tq
    n_full = q0 // tk
    lane = lax.broadcasted_iota(jnp.int32, (1, LANES), 1)
    c_blk = cq_ref[0]
    qms, cq_ps = [], []
    for g in range(groups):
        q2 = q_ref[0, :, g * LANES:(g + 1) * LANES]
        qms.append([jnp.where(m, q2, jnp.zeros_like(q2)) for m in masks])
        head0 = (hp * groups + g) * HEADS_PER_STEP
        cqs = [jnp.sum(jnp.where(lane == head0 + hh, c_blk, 0.0), axis=1, keepdims=True)
               for hh in range(HEADS_PER_STEP)]
        cq_ps.append(jnp.where(masks[0], cqs[0], cqs[1]))

    def step(k_start, width, masked, carry):
        k_start = pl.multiple_of(k_start, LANES)
        if masked:
            own = (lax.broadcasted_iota(jnp.int32, (tq, tq), 1) <= lax.broadcasted_iota(jnp.int32, (tq, tq), 0))

        def causal(w):
            own_w = jnp.where(own, w[:, width - tq:], -jnp.inf)
            return own_w if width == tq else jnp.concatenate([w[:, :width - tq], own_w], axis=1)
        ws = []
        for g in range(groups):
            k2 = k_ref[0, pl.ds(k_start, width), g * LANES:(g + 1) * LANES]
            for hh in range(HEADS_PER_STEP):
                head = (hp * groups + g) * HEADS_PER_STEP + hh
                ck = ck_ref[0, pl.ds(head, 1), pl.ds(k_start, width)]
                w = _qk(qms[g][hh], k2) - ck
                ws.append(causal(w) if masked else w)
        out = []
        for g in range(groups):
            m_p, l_p, acc = carry[g]
            w0, w1 = ws[g * HEADS_PER_STEP], ws[g * HEADS_PER_STEP + 1]
            mx_p = jnp.where(masks[0], jnp.max(w0, axis=1, keepdims=True), jnp.max(w1, axis=1, keepdims=True))
            m_new = jnp.maximum(m_p, cq_ps[g] + mx_p)
            shift = m_new - cq_ps[g]
            alpha = jnp.exp(m_p - m_new)
            p0 = jnp.exp(w0 - shift[:, 0:1])
            p1 = jnp.exp(w1 - shift[:, HEAD_DIM:HEAD_DIM + 1])
            sum_p = jnp.where(masks[0], jnp.sum(p0, axis=1, keepdims=True), jnp.sum(p1, axis=1, keepdims=True))
            v2 = v_ref[0, pl.ds(k_start, width), g * LANES:(g + 1) * LANES]
            vv = jnp.concatenate([jnp.where(m, v2, jnp.zeros_like(v2)) for m in masks], axis=0)
            acc = acc * alpha + jnp.dot(jnp.concatenate([p0.astype(bf16), p1.astype(bf16)], axis=1), vv,
                                        preferred_element_type=f32)
            out.append((m_new, alpha * l_p + sum_p, acc))
        return tuple(out)

    init = tuple((jnp.full((tq, LANES), -jnp.inf, f32), jnp.zeros((tq, LANES), f32),
                  jnp.zeros((tq, LANES), f32)) for _ in range(groups))
    c = init if tk == k_ref.shape[1] else lax.fori_loop(0, n_full, lambda kb, c: step(kb * tk, tk, False, c), init)
    variants = [functools.partial(step, n_full * tk, (j + 1) * tq, True) for j in range(tk // tq)]
    c = variants[0](c) if len(variants) == 1 else lax.switch((q0 - n_full * tk) // tq, variants, c)
    o_ref[0] = jnp.concatenate([acc / l_p for _, l_p, acc in c], axis=1).astype(o_ref.dtype)


def attention(qkv, batch, seq, c=None):
    d = qkv.shape[1] // 3
    width = LANES * ATTN_LANE_GROUPS
    nlb = d // width
    tq = min(ATTN_BLOCK, seq)
    qkv3 = qkv.reshape(batch, seq, 3 * d)
    in_specs = [pl.BlockSpec((1, tq, width), lambda b, hp, qi: (b, qi, hp)),
                pl.BlockSpec((1, seq, width), lambda b, hp, qi: (b, 0, nlb + hp)),
                pl.BlockSpec((1, seq, width), lambda b, hp, qi: (b, 0, 2 * nlb + hp))]
    args = [qkv3, qkv3, qkv3]
    if c is None:
        kern = _sb_attn_kernel
        u = (jnp.arange(tq)[:, None] >= jnp.arange(tq)[None, :]).astype(bf16)
        in_specs.append(pl.BlockSpec((2 * tq, tq), lambda b, hp, qi: (0, 0)))
        args.append(jnp.concatenate([u, u], axis=0))
    else:
        kern = _fox_attn_kernel
        tk = min(FOX_KEY_BLOCK, seq)
        assert tk % tq == 0 and seq % tk == 0
        ch = jnp.transpose(c[:, :, :N_HEADS], (0, 2, 1))
        in_specs.append(pl.BlockSpec((1, tq, LANES), lambda b, hp, qi: (b, qi, 0)))
        in_specs.append(pl.BlockSpec((1, N_HEADS, seq), lambda b, hp, qi: (b, 0, 0)))
        args += [c, ch]
    out = pl.pallas_call(
        kern, grid=(batch, nlb, seq // tq), in_specs=in_specs,
        out_specs=pl.BlockSpec((1, tq, width), lambda b, hp, qi: (b, qi, hp)),
        out_shape=jax.ShapeDtypeStruct((batch, seq, d), bf16),
        compiler_params=_params("parallel", "parallel", "arbitrary"), name="attention")(*args)
    return out.reshape(batch * seq, d)


def _oproj_router_kernel(o_ref, h_ref, wo_ref, g_ref, wrh_ref, wrl_ref, br_ref, tri_ref,
                         h1_ref, xn_ref, route_ref, cnt_ref, carry_ref):
    @pl.when(pl.program_id(0) == 0)
    def _():
        carry_ref[...] = jnp.zeros_like(carry_ref)

    tm = h_ref.shape[0]
    h1 = h_ref[...] + jnp.dot(o_ref[...], wo_ref[...], preferred_element_type=f32)
    h1_ref[...] = h1
    xn = _rmsnorm(h1, g_ref[...])
    xn_ref[...] = xn

    xh, xl = _split2(xn)
    logits = (jnp.dot(xh, wrh_ref[...], preferred_element_type=f32)
              + jnp.dot(xh, wrl_ref[...], preferred_element_type=f32)
              + jnp.dot(xl, wrh_ref[...], preferred_element_type=f32)) + br_ref[...]
    lane = lax.broadcasted_iota(jnp.int32, (tm, LANES), 1)
    lane_f = lane.astype(f32)
    work = jnp.where(lane < N_EXPERTS, logits, -jnp.inf)

    vals, idxs, hots = [], [], []
    for _ in range(TOP_K):
        mx = jnp.max(work, axis=1, keepdims=True)
        idx = jnp.min(jnp.where(work == mx, lane_f, float(LANES)), axis=1, keepdims=True)
        hot = lane_f == idx
        work = jnp.where(hot, -jnp.inf, work)
        vals.append(mx)
        idxs.append(idx)
        hots.append(hot)

    exps = [jnp.exp(v - vals[0]) for v in vals]
    denom = exps[0] + exps[1] + exps[2] + exps[3]
    gates = [e / denom for e in exps]

    sel = jnp.zeros((tm, LANES), f32)
    for hot in hots:
        sel = sel + hot.astype(f32)
    before = jnp.dot(tri_ref[...], sel.astype(bf16), preferred_element_type=f32) + carry_ref[...]
    ranks = [jnp.sum(jnp.where(hot, before, 0.0), axis=1, keepdims=True) for hot in hots]
    carry = carry_ref[...] + jnp.sum(sel, axis=0, keepdims=True)
    carry_ref[...] = carry
    cnt_ref[...] = jnp.broadcast_to(carry, cnt_ref.shape)

    route = jnp.zeros((tm, LANES), f32)
    for base, cols in ((ROUTE_IDX, idxs), (ROUTE_GATE, gates), (ROUTE_RANK, ranks)):
        for k, colv in enumerate(cols):
            route = jnp.where(lane == base + k, colv, route)
    route_ref[...] = route


def oproj_router(o, h, wo, g, wr_hi, wr_lo, br):
    n, d = h.shape
    tm = min(TOKEN_BLOCK, n)
    tri = (jnp.arange(tm)[None, :] < jnp.arange(tm)[:, None]).astype(bf16)
    row = lambda i: (i, 0)
    fixed = lambda i: (0, 0)
    return pl.pallas_call(
        _oproj_router_kernel,
        grid=(n // tm,),
        in_specs=[pl.BlockSpec((tm, d), row), pl.BlockSpec((tm, d), row),
                  pl.BlockSpec((d, d), fixed), pl.BlockSpec((1, d), fixed),
                  pl.BlockSpec((d, LANES), fixed), pl.BlockSpec((d, LANES), fixed),
                  pl.BlockSpec((1, LANES), fixed), pl.BlockSpec((tm, tm), fixed)],
        out_specs=[pl.BlockSpec((tm, d), row), pl.BlockSpec((tm, d), row),
                   pl.BlockSpec((tm, LANES), row), pl.BlockSpec((8, LANES), fixed)],
        out_shape=[jax.ShapeDtypeStruct((n, d), f32), jax.ShapeDtypeStruct((n, d), f32),
                   jax.ShapeDtypeStruct((n, LANES), f32), jax.ShapeDtypeStruct((8, LANES), f32)],
        scratch_shapes=[pltpu.VMEM((1, LANES), f32)],
        compiler_params=_params("arbitrary"), name="oproj_router",
    )(o, h, wo, g.reshape(1, d), wr_hi, wr_lo, br, tri)


def _dispatch_kernel(zlo_ref, zhi_ref, pos_ref, x_ref, rows_ref, zero_ref, sem):
    tm = x_ref.shape[0]

    def row_copy(src, r):
        return pltpu.make_async_copy(src, rows_ref.at[pl.ds(r, 1)], sem)

    @pl.when(pl.program_id(0) == 0)
    def _():
        zero_ref[...] = jnp.zeros_like(zero_ref)
        zrows = zero_ref.shape[0]

        def chunk(start, size, wait):
            if size > 1:
                start = pl.multiple_of(start, SUBLANES)
            cp = pltpu.make_async_copy(zero_ref.at[pl.ds(0, size)], rows_ref.at[pl.ds(start, size)], sem)
            cp.wait() if wait else cp.start()

        for wait in (False, True):
            def per_expert(e, _, wait=wait):
                hi = zhi_ref[e]
                aligned = jnp.minimum(hi, (zlo_ref[e] + SUBLANES - 1) // SUBLANES * SUBLANES)

                def single(r, _):
                    chunk(r, 1, wait)
                    return 0
                lax.fori_loop(zlo_ref[e], aligned, single, 0)
                length = hi - aligned
                n_whole = length // zrows

                def whole(j, _):
                    chunk(aligned + j * zrows, zrows, wait)
                    return 0
                lax.fori_loop(0, n_whole, whole, 0)
                start = aligned + n_whole * zrows
                size = zrows // 2
                while size >= SUBLANES:
                    has = (length & size) != 0

                    @pl.when(has)
                    def _(start=start, size=size):
                        chunk(start, size, wait)
                    start = start + jnp.where(has, size, 0)
                    size //= 2
                return 0
            lax.fori_loop(0, N_EXPERTS, per_expert, 0)

    for wait in (False, True):
        def per_token(t, _, wait=wait):
            for k in range(TOP_K):
                cp = row_copy(x_ref.at[pl.ds(t, 1)], pos_ref[t * TOP_K + k])
                cp.wait() if wait else cp.start(priority=k % 2)
            return 0
        lax.fori_loop(0, tm, per_token, 0, unroll=DMA_LOOP_UNROLL)


def dispatch(xn, pos_flat, zero_lo, zero_hi, n_rows):
    n, d = xn.shape
    tm = min(SCATTER_BLOCK, n)
    grid_spec = pltpu.PrefetchScalarGridSpec(
        num_scalar_prefetch=2, grid=(n // tm,),
        in_specs=[pl.BlockSpec((tm * TOP_K,), lambda i, *_: (i,), memory_space=pltpu.SMEM),
                  pl.BlockSpec((tm, d), lambda i, *_: (i, 0))],
        out_specs=pl.BlockSpec(memory_space=pl.ANY),
        scratch_shapes=[pltpu.VMEM((ROW_BLOCK, d), f32), pltpu.SemaphoreType.DMA(())])
    return pl.pallas_call(
        _dispatch_kernel, grid_spec=grid_spec,
        out_shape=jax.ShapeDtypeStruct((n_rows, d), f32),
        compiler_params=_params("arbitrary"), name="dispatch",
    )(zero_lo, zero_hi, pos_flat, xn)


def _experts_kernel(be_ref, nused_ref, x_ref, wgu_ref, bgu_ref, wd_ref, bd_ref, y_ref, wgu_b, wd_b):
    rb = pl.program_id(0)
    d_ff = wd_ref.shape[2]
    used = rb < nused_ref[0]
    new_expert = (rb == 0) | (be_ref[rb] != be_ref[jnp.maximum(rb - 1, 0)])

    @pl.when(used & new_expert)
    def _():
        wgu_b[...] = wgu_ref[0, 0].astype(bf16)
        wd_b[...] = wd_ref[0, 0].astype(bf16)

    @pl.when(used)
    def _():
        x = x_ref[...].astype(bf16)
        hgu = jnp.dot(x, wgu_b[...], preferred_element_type=f32) + bgu_ref[0, 0]
        glu = jnp.minimum(hgu[:, :d_ff], SWIGLU_LIMIT)
        lin = jnp.clip(hgu[:, d_ff:], -SWIGLU_LIMIT, SWIGLU_LIMIT)
        act = glu * (1.0 / (1.0 + jnp.exp(-SWIGLU_ALPHA * glu))) * (lin + 1.0)
        y_ref[...] = jnp.dot(act.astype(bf16), wd_b[...], preferred_element_type=f32) + bd_ref[0, 0]

    @pl.when(jnp.logical_not(used))
    def _():
        y_ref[...] = jnp.zeros_like(y_ref)


def experts(x_rows, block_expert, n_used, layer, w_gu, b_gu, w_down, b_down):
    n_rows, d = x_rows.shape
    f2 = w_gu.shape[3]
    d_ff = f2 // 2
    wmap = lambda rb, be, nu: (layer, be[rb], 0, 0)
    rows = pl.BlockSpec((ROW_BLOCK, d), lambda rb, be, nu: (rb, 0))
    grid_spec = pltpu.PrefetchScalarGridSpec(
        num_scalar_prefetch=2, grid=(n_rows // ROW_BLOCK,),
        in_specs=[rows, pl.BlockSpec((1, 1, d, f2), wmap), pl.BlockSpec((1, 1, 1, f2), wmap),
                  pl.BlockSpec((1, 1, d_ff, d), wmap), pl.BlockSpec((1, 1, 1, d), wmap)],
        out_specs=rows,
        scratch_shapes=[pltpu.VMEM((d, f2), bf16), pltpu.VMEM((d_ff, d), bf16)])
    return pl.pallas_call(
        _experts_kernel, grid_spec=grid_spec,
        out_shape=jax.ShapeDtypeStruct((n_rows, d), f32),
        compiler_params=pltpu.CompilerParams(dimension_semantics=("arbitrary",),
                                             vmem_limit_bytes=EXPERTS_VMEM_LIMIT_BYTES),
        name="experts",
    )(block_expert, n_used, x_rows, w_gu, b_gu, w_down, b_down)


def _combine_ple_kernel(pos_ref, pos_next_ref, route_ref, h_ref, p_ref, g_ref, wg_ref, wp_ref, gf_ref,
                        yrows_ref, o_ref, ybuf0_ref, ybuf1_ref, sems, *, final):
    i = pl.program_id(0)
    tm = h_ref.shape[0]

    def row_copy(idx_ref, buf, s, t, k):
        r = 0 if idx_ref is None else idx_ref[t * TOP_K + k]
        return pltpu.make_async_copy(yrows_ref.at[pl.ds(r, 1)], buf.at[k, pl.ds(t, 1)], sems.at[s])

    def start_rows(idx_ref, buf, s, tokens):
        for t in tokens:
            for k in range(TOP_K):
                row_copy(idx_ref, buf, s, t, k).start(priority=k % 2)

    def wait_rows(buf, s):
        def per_token(t, _):
            for k in range(TOP_K):
                row_copy(None, buf, s, t, k).wait()
            return 0
        lax.fori_loop(0, tm, per_token, 0, unroll=DMA_LOOP_UNROLL)

    @pl.when(i == 0)
    def _():
        def per_token(t, _):
            start_rows(pos_ref, ybuf0_ref, 0, [t])
            return 0
        lax.fori_loop(0, tm, per_token, 0, unroll=DMA_LOOP_UNROLL)

    def step(cur, nxt, s_cur, s_nxt):
        wait_rows(cur, s_cur)
        start_rows(pos_next_ref, nxt, s_nxt, range(tm))
        proj = jnp.dot(p_ref[0].astype(bf16), wp_ref[...], preferred_element_type=f32)
        route = route_ref[...]
        h2 = h_ref[...]
        for k in range(TOP_K):
            h2 = h2 + route[:, ROUTE_GATE + k:ROUTE_GATE + k + 1] * cur[k]
        hn = _rmsnorm(h2, g_ref[...]).astype(bf16)
        gate = 1.0 / (1.0 + jnp.exp(-jnp.dot(hn, wg_ref[...], preferred_element_type=f32)))
        h3 = h2 + gate * proj
        if final:
            h3 = _rmsnorm(h3, gf_ref[...])
        o_ref[...] = h3

        @pl.when(i == pl.num_programs(0) - 1)
        def _():
            wait_rows(nxt, s_nxt)

    @pl.when(i % 2 == 0)
    def _():
        step(ybuf0_ref, ybuf1_ref, 0, 1)

    @pl.when(i % 2 == 1)
    def _():
        step(ybuf1_ref, ybuf0_ref, 1, 0)


def combine_ple(pos_flat, route, h1, y_rows, p, layer, g, wg, wp, g_final, final):
    n, d = h1.shape
    pd = p.shape[2]
    tm = min(GATHER_BLOCK, n)
    last = n // tm - 1
    row = lambda i: (i, 0)
    fixed = lambda i: (0, 0)
    return pl.pallas_call(
        functools.partial(_combine_ple_kernel, final=final),
        grid=(n // tm,),
        in_specs=[pl.BlockSpec((tm * TOP_K,), lambda i: (i,), memory_space=pltpu.SMEM),
                  pl.BlockSpec((tm * TOP_K,), lambda i: (jnp.minimum(i + 1, last),), memory_space=pltpu.SMEM),
                  pl.BlockSpec((tm, LANES), row), pl.BlockSpec((tm, d), row),
                  pl.BlockSpec((1, tm, pd), lambda i: (layer, i, 0)), pl.BlockSpec((1, d), fixed),
                  pl.BlockSpec((d, d), fixed), pl.BlockSpec((pd, d), fixed),
                  pl.BlockSpec((1, d), fixed), pl.BlockSpec(memory_space=pl.ANY)],
        out_specs=pl.BlockSpec((tm, d), row),
        out_shape=jax.ShapeDtypeStruct((n, d), f32),
        scratch_shapes=[pltpu.VMEM((TOP_K, tm, d), f32), pltpu.VMEM((TOP_K, tm, d), f32),
                        pltpu.SemaphoreType.DMA((2,))],
        compiler_params=_params("arbitrary"), name="combine_ple",
    )(pos_flat, pos_flat, route, h1, p, g.reshape(1, d), wg, wp, g_final.reshape(1, d), y_rows)


def _routing_tables(route, counts_f, n):
    counts = counts_f[0, :N_EXPERTS].astype(jnp.int32)
    padded = (counts + ROW_BLOCK - 1) // ROW_BLOCK * ROW_BLOCK
    pad_end = jnp.cumsum(padded)
    pad_start = pad_end - padded
    n_blocks = -(-(n * TOP_K + N_EXPERTS * (ROW_BLOCK - 1)) // ROW_BLOCK)
    n_rows = n_blocks * ROW_BLOCK
    idx = route[:, ROUTE_IDX:ROUTE_IDX + TOP_K].astype(jnp.int32)
    rank = route[:, ROUTE_RANK:ROUTE_RANK + TOP_K].astype(jnp.int32)
    pos_flat = (pad_start[idx] + rank).reshape(-1)
    block_start = jnp.arange(n_blocks, dtype=jnp.int32) * ROW_BLOCK
    block_expert = jnp.minimum(jnp.sum(block_start[:, None] >= pad_end[None, :], axis=1),
                               N_EXPERTS - 1).astype(jnp.int32)
    n_used = (pad_end[-1:] // ROW_BLOCK).astype(jnp.int32)
    zero_lo = (pad_start + counts).astype(jnp.int32)
    zero_hi = jnp.concatenate([pad_start[1:], jnp.array([n_rows], jnp.int32)]).astype(jnp.int32)
    return pos_flat, block_expert, n_used, zero_lo, zero_hi, n_rows


def kernel(x, p, g_mix, w_qkv, w_fgate, b_fgate, w_o, g_moe, w_router, b_router, w_gu, b_gu,
           w_down, b_down, g_ple, w_ple_gate, w_ple_proj, g_final):
    batch, seq, d = x.shape
    depth = w_qkv.shape[0]
    n = batch * seq
    h = x.reshape(n, d)
    q_scale = jnp.concatenate([jnp.full((d,), HEAD_DIM ** -0.5, f32), jnp.ones((2 * d,), f32)])
    n_exp = w_gu.shape[1]
    b_gu4 = b_gu.reshape(depth, n_exp, 1, -1)
    b_down4 = b_down.reshape(depth, n_exp, 1, -1)
    for i in range(depth):
        wqkv = (w_qkv[i] * q_scale).astype(bf16)
        if i % 2 == 0:
            qkv = norm_proj(h, g_mix[i], wqkv)
            o = attention(qkv, batch, seq)
        else:
            j = i // 2
            wf = jnp.pad(w_fgate[j], ((0, 0), (0, LANES - N_HEADS))).astype(bf16)
            bf = jnp.pad(b_fgate[j], (0, LANES - N_HEADS)).reshape(1, LANES)
            qkv, f_logit = norm_proj(h, g_mix[i], wqkv, wf)
            c = fox_cumsum(f_logit, bf, batch, seq)
            o = attention(qkv, batch, seq, c)

        wr = jnp.pad(w_router[i], ((0, 0), (0, LANES - N_EXPERTS)))
        wr_hi = wr.astype(bf16)
        wr_lo = (wr - wr_hi.astype(f32)).astype(bf16)
        br = jnp.pad(b_router[i], (0, LANES - N_EXPERTS)).reshape(1, LANES)
        h1, xn, route, counts = oproj_router(o, h, w_o[i].astype(bf16), g_moe[i], wr_hi, wr_lo, br)

        pos_flat, block_expert, n_used, zero_lo, zero_hi, n_rows = _routing_tables(route, counts, n)
        x_rows = dispatch(xn, pos_flat, zero_lo, zero_hi, n_rows)
        y_rows = experts(x_rows, block_expert, n_used, i, w_gu, b_gu4, w_down, b_down4)
        h = combine_ple(pos_flat, route, h1, y_rows, p.reshape(depth, n, -1), i, g_ple[i],
                        w_ple_gate[i].astype(bf16), w_ple_proj[i].astype(bf16), g_final,
                        final=(i == depth - 1))
    return h.reshape(batch, seq, d)
```

```python
import functools

import jax
import jax.numpy as jnp
from jax import lax
from jax.experimental import pallas as pl
from jax.experimental.pallas import tpu as pltpu

N_HEADS = 16
HEAD_DIM = 64
N_EXPERTS = 32
TOP_K = 4
SWIGLU_LIMIT = 7.0
SWIGLU_ALPHA = 1.702
RMS_EPS = 1e-6

LANES = 128
SUBLANES = 8
HEADS_PER_STEP = LANES // HEAD_DIM
VMEM_LIMIT_BYTES = 48 * 1024 * 1024
EXPERTS_VMEM_LIMIT_BYTES = 56 * 1024 * 1024

ROW_BLOCK = 512
ATTN_BLOCK = 256
TOKEN_BLOCK = 512
GATHER_BLOCK = 256
SCATTER_BLOCK = 512

SB_SKIP_BELOW = -104.0
DMA_LOOP_UNROLL = 8
ATTN_LANE_GROUPS = 2
FOX_KEY_BLOCK = 2048

ROUTE_IDX, ROUTE_GATE, ROUTE_RANK = 0, TOP_K, 2 * TOP_K

f32 = jnp.float32
bf16 = jnp.bfloat16


def _params(*semantics):
    return pltpu.CompilerParams(dimension_semantics=semantics, vmem_limit_bytes=VMEM_LIMIT_BYTES)


def _rmsnorm(x, g):
    var = jnp.mean(x * x, axis=-1, keepdims=True)
    return x * lax.rsqrt(var + RMS_EPS) * g


def _neg_abs(x):
    bits = lax.bitcast_convert_type(x, jnp.uint32) | jnp.uint32(0x80000000)
    return lax.bitcast_convert_type(bits, f32)


def _split2(x):
    hi = x.astype(bf16)
    lo = (x - hi.astype(f32)).astype(bf16)
    return hi, lo


def _norm_proj_kernel(h_ref, g_ref, w_ref, *rest, with_gate):
    xn = _rmsnorm(h_ref[...], g_ref[...]).astype(bf16)
    if with_gate:
        wf_ref, o_ref, of_ref = rest
        of_ref[...] = jnp.dot(xn, wf_ref[...], preferred_element_type=f32)
    else:
        (o_ref,) = rest
    o_ref[...] = jnp.dot(xn, w_ref[...], preferred_element_type=f32).astype(o_ref.dtype)


def norm_proj(h, g, w, wf=None):
    n, d = h.shape
    m = w.shape[1]
    tm = min(TOKEN_BLOCK, n)
    with_gate = wf is not None
    in_specs = [pl.BlockSpec((tm, d), lambda i: (i, 0)),
                pl.BlockSpec((1, d), lambda i: (0, 0)),
                pl.BlockSpec((d, m), lambda i: (0, 0))]
    out_specs = [pl.BlockSpec((tm, m), lambda i: (i, 0))]
    out_shape = [jax.ShapeDtypeStruct((n, m), bf16)]
    args = [h, g.reshape(1, d), w]
    if with_gate:
        in_specs.append(pl.BlockSpec((d, LANES), lambda i: (0, 0)))
        out_specs.append(pl.BlockSpec((tm, LANES), lambda i: (i, 0)))
        out_shape.append(jax.ShapeDtypeStruct((n, LANES), f32))
        args.append(wf)
    out = pl.pallas_call(
        functools.partial(_norm_proj_kernel, with_gate=with_gate),
        grid=(n // tm,), in_specs=in_specs, out_specs=out_specs, out_shape=out_shape,
        compiler_params=_params("parallel"), name="norm_proj")(*args)
    return out if with_gate else out[0]


def _fox_cumsum_kernel(f_ref, b_ref, tri_ref, c_ref, carry_ref):
    @pl.when(pl.program_id(1) == 0)
    def _():
        carry_ref[...] = jnp.zeros_like(carry_ref)

    x = f_ref[0] + b_ref[...]
    log_f = jnp.minimum(x, 0.0) - jnp.log(1.0 + jnp.exp(-jnp.abs(x)))
    p1 = log_f.astype(bf16)
    r1 = log_f - p1.astype(f32)
    p2 = r1.astype(bf16)
    p3 = (r1 - p2.astype(f32)).astype(bf16)
    tri = tri_ref[...]
    c = (jnp.dot(tri, p1, preferred_element_type=f32)
         + jnp.dot(tri, p2, preferred_element_type=f32)
         + jnp.dot(tri, p3, preferred_element_type=f32)) + carry_ref[...]
    c_ref[0] = c
    carry_ref[...] = c[-1:, :]


def fox_cumsum(f_logit, b_pad, batch, seq):
    ts = min(ATTN_BLOCK, seq)
    tri = (jnp.arange(ts)[None, :] <= jnp.arange(ts)[:, None]).astype(bf16)
    return pl.pallas_call(
        _fox_cumsum_kernel,
        grid=(batch, seq // ts),
        in_specs=[pl.BlockSpec((1, ts, LANES), lambda b, s: (b, s, 0)),
                  pl.BlockSpec((1, LANES), lambda b, s: (0, 0)),
                  pl.BlockSpec((ts, ts), lambda b, s: (0, 0))],
        out_specs=pl.BlockSpec((1, ts, LANES), lambda b, s: (b, s, 0)),
        out_shape=jax.ShapeDtypeStruct((batch, seq, LANES), f32),
        scratch_shapes=[pltpu.VMEM((1, LANES), f32)],
        compiler_params=_params("parallel", "arbitrary"), name="fox_cumsum",
    )(f_logit.reshape(batch, seq, LANES), b_pad, tri)


def _head_mask(hh):
    lane = lax.broadcasted_iota(jnp.int32, (1, LANES), 1)
    return (lane >= hh * HEAD_DIM) & (lane < (hh + 1) * HEAD_DIM)


def _qk(qm, k2):
    return lax.dot_general(qm, k2, (((1,), (1,)), ((), ())), preferred_element_type=f32)


def _sb_attn_kernel(q_ref, k_ref, v_ref, u_ref, o_ref):
    qi = pl.program_id(2)
    tq = q_ref.shape[1]
    tk = tq
    groups = q_ref.shape[2] // LANES
    u2 = u_ref[...]
    masks = [_head_mask(hh) for hh in range(HEADS_PER_STEP)]
    neg_qs = []
    for g in range(groups):
        q2 = q_ref[0, :, g * LANES:(g + 1) * LANES]
        neg_qs += [jnp.where(m, -q2, jnp.zeros_like(q2)) for m in masks]
    n_heads = len(neg_qs)

    def two_blocks(kb, runs, accs, diag):
        has_next = kb >= 1
        kbs = [kb, jnp.maximum(kb - 1, 0)]
        if diag:
            strict = (lax.broadcasted_iota(jnp.int32, (tq, tk), 1)
                      < lax.broadcasted_iota(jnp.int32, (tq, tk), 0))
        chains = [(blk, h) for blk in range(2) for h in range(n_heads)]

        def keys(ref, blk, h):
            g = h // HEADS_PER_STEP
            return ref[0, pl.ds(kbs[blk] * tk, tk), g * LANES:(g + 1) * LANES]

        nzs = {c: _qk(neg_qs[c[1]], keys(k_ref, *c)) for c in chains}
        suffixes = {}
        for c in chains:
            nz = nzs[c]
            log_not = jnp.minimum(nz, 0.0) - jnp.log(1.0 + jnp.exp(_neg_abs(nz)))
            if diag and c[0] == 0:
                log_not = jnp.where(strict, log_not, 0.0)
            hi, lo = _split2(log_not)
            suffixes[c] = jnp.dot(jnp.concatenate([hi, lo], axis=1), u2, preferred_element_type=f32)
        weights, new_runs = {}, []
        for h in range(n_heads):
            after = runs[h] + suffixes[(0, h)][:, 0:1]
            new_runs.append(jnp.where(has_next, after + suffixes[(1, h)][:, 0:1], after))
            for blk, run in ((0, runs[h]), (1, jnp.where(has_next, after, -jnp.inf))):
                a = jnp.exp((run - nzs[(blk, h)]) + suffixes[(blk, h)])
                if diag and blk == 0:
                    a = jnp.where(strict, a, 0.0)
                weights[(blk, h)] = a.astype(bf16)
        new_accs = []
        for g in range(groups):
            hs = range(g * HEADS_PER_STEP, (g + 1) * HEADS_PER_STEP)
            v2s = [keys(v_ref, blk, g * HEADS_PER_STEP) for blk in range(2)]
            vv = jnp.concatenate([jnp.where(m, v2, jnp.zeros_like(v2)) for v2 in v2s for m in masks], axis=0)
            ww = jnp.concatenate([weights[(blk, h)] for blk in range(2) for h in hs], axis=1)
            new_accs.append(accs[g] + jnp.dot(ww, vv, preferred_element_type=f32))
        return new_runs, new_accs

    def alive(runs):
        top = functools.reduce(jnp.maximum, runs)
        return (jnp.max(top) > SB_SKIP_BELOW).astype(jnp.int32)

    zero_run = jnp.zeros((tq, 1), f32)
    runs, accs = two_blocks(qi, [zero_run] * n_heads, [jnp.zeros((tq, LANES), f32)] * groups, True)

    def cond(c):
        return (c[0] >= 0) & (c[1] > 0)

    def body(c):
        runs, accs = two_blocks(c[0], list(c[2]), list(c[3]), False)
        return c[0] - 2, alive(runs), tuple(runs), tuple(accs)

    accs = lax.while_loop(cond, body, (qi - 2, alive(runs), tuple(runs), tuple(accs)))[3]
    o_ref[0] = jnp.concatenate(list(accs), axis=1).astype(o_ref.dtype)


def _fox_attn_kernel(q_ref, k_ref, v_ref, cq_ref, ck_ref, o_ref):
    hp = pl.program_id(1)
    qi = pl.program_id(2)
    tq = q_ref.shape[1]
    tk = min(FOX_KEY_BLOCK, k_ref.shape[1])
    groups = q_ref.shape[2] // LANES
    masks = [_head_mask(hh) for hh in range(HEADS_PER_STEP)]
    q0 = qi * tq
    n_full = q0 // tk
    lane = lax.broadcasted_iota(jnp.int32, (1, LANES), 1)
    c_blk = cq_ref[0]
    qms, cq_ps = [], []
    for g in range(groups):
        q2 = q_ref[0, :, g * LANES:(g + 1) * LANES]
        qms.append([jnp.where(m, q2, jnp.zeros_like(q2)) for m in masks])
        head0 = (hp * groups + g) * HEADS_PER_STEP
        cqs = [jnp.sum(jnp.where(lane == head0 + hh, c_blk, 0.0), axis=1, keepdims=True)
               for hh in range(HEADS_PER_STEP)]
        cq_ps.append(jnp.where(masks[0], cqs[0], cqs[1]))

    def step(k_start, width, masked, carry):
        k_start = pl.multiple_of(k_start, LANES)
        if masked:
            own = (lax.broadcasted_iota(jnp.int32, (tq, tq), 1) <= lax.broadcasted_iota(jnp.int32, (tq, tq), 0))

        def causal(w):
            own_w = jnp.where(own, w[:, width - tq:], -jnp.inf)
            return own_w if width == tq else jnp.concatenate([w[:, :width - tq], own_w], axis=1)
        ws = []
        for g in range(groups):
            k2 = k_ref[0, pl.ds(k_start, width), g * LANES:(g + 1) * LANES]
            for hh in range(HEADS_PER_STEP):
                head = (hp * groups + g) * HEADS_PER_STEP + hh
                ck = ck_ref[0, pl.ds(head, 1), pl.ds(k_start, width)]
                w = _qk(qms[g][hh], k2) - ck
                ws.append(causal(w) if masked else w)
        out = []
        for g in range(groups):
            m_p, l_p, acc = carry[g]
            w0, w1 = ws[g * HEADS_PER_STEP], ws[g * HEADS_PER_STEP + 1]
            mx_p = jnp.where(masks[0], jnp.max(w0, axis=1, keepdims=True), jnp.max(w1, axis=1, keepdims=True))
            m_new = jnp.maximum(m_p, cq_ps[g] + mx_p)
            shift = m_new - cq_ps[g]
            alpha = jnp.exp(m_p - m_new)
            p0 = jnp.exp(w0 - shift[:, 0:1])
            p1 = jnp.exp(w1 - shift[:, HEAD_DIM:HEAD_DIM + 1])
            sum_p = jnp.where(masks[0], jnp.sum(p0, axis=1, keepdims=True), jnp.sum(p1, axis=1, keepdims=True))
            v2 = v_ref[0, pl.ds(k_start, width), g * LANES:(g + 1) * LANES]
            vv = jnp.concatenate([jnp.where(m, v2, jnp.zeros_like(v2)) for m in masks], axis=0)
            acc = acc * alpha + jnp.dot(jnp.concatenate([p0.astype(bf16), p1.astype(bf16)], axis=1), vv,
                                        preferred_element_type=f32)
            out.append((m_new, alpha * l_p + sum_p, acc))
        return tuple(out)

    init = tuple((jnp.full((tq, LANES), -jnp.inf, f32), jnp.zeros((tq, LANES), f32),
                  jnp.zeros((tq, LANES), f32)) for _ in range(groups))
    c = init if tk == k_ref.shape[1] else lax.fori_loop(0, n_full, lambda kb, c: step(kb * tk, tk, False, c), init)
    variants = [functools.partial(step, n_full * tk, (j + 1) * tq, True) for j in range(tk // tq)]
    c = variants[0](c) if len(variants) == 1 else lax.switch((q0 - n_full * tk) // tq, variants, c)
    o_ref[0] = jnp.concatenate([acc / l_p for _, l_p, acc in c], axis=1).astype(o_ref.dtype)


def attention(qkv, batch, seq, c=None):
    d = qkv.shape[1] // 3
    width = LANES * ATTN_LANE_GROUPS
    nlb = d // width
    tq = min(ATTN_BLOCK, seq)
    qkv3 = qkv.reshape(batch, seq, 3 * d)
    in_specs = [pl.BlockSpec((1, tq, width), lambda b, hp, qi: (b, qi, hp)),
                pl.BlockSpec((1, seq, width), lambda b, hp, qi: (b, 0, nlb + hp)),
                pl.BlockSpec((1, seq, width), lambda b, hp, qi: (b, 0, 2 * nlb + hp))]
    args = [qkv3, qkv3, qkv3]
    if c is None:
        kern = _sb_attn_kernel
        u = (jnp.arange(tq)[:, None] >= jnp.arange(tq)[None, :]).astype(bf16)
        in_specs.append(pl.BlockSpec((2 * tq, tq), lambda b, hp, qi: (0, 0)))
        args.append(jnp.concatenate([u, u], axis=0))
    else:
        kern = _fox_attn_kernel
        tk = min(FOX_KEY_BLOCK, seq)
        assert tk % tq == 0 and seq % tk == 0
        ch = jnp.transpose(c[:, :, :N_HEADS], (0, 2, 1))
        in_specs.append(pl.BlockSpec((1, tq, LANES), lambda b, hp, qi: (b, qi, 0)))
        in_specs.append(pl.BlockSpec((1, N_HEADS, seq), lambda b, hp, qi: (b, 0, 0)))
        args += [c, ch]
    out = pl.pallas_call(
        kern, grid=(batch, nlb, seq // tq), in_specs=in_specs,
        out_specs=pl.BlockSpec((1, tq, width), lambda b, hp, qi: (b, qi, hp)),
        out_shape=jax.ShapeDtypeStruct((batch, seq, d), bf16),
        compiler_params=_params("parallel", "parallel", "arbitrary"), name="attention")(*args)
    return out.reshape(batch * seq, d)


def _oproj_router_kernel(o_ref, h_ref, wo_ref, g_ref, wrh_ref, wrl_ref, br_ref, tri_ref,
                         h1_ref, xn_ref, route_ref, cnt_ref, carry_ref):
    @pl.when(pl.program_id(0) == 0)
    def _():
        carry_ref[...] = jnp.zeros_like(carry_ref)

    tm = h_ref.shape[0]
    h1 = h_ref[...] + jnp.dot(o_ref[...], wo_ref[...], preferred_element_type=f32)
    h1_ref[...] = h1
    xn = _rmsnorm(h1, g_ref[...])
    xn_ref[...] = xn

    xh, xl = _split2(xn)
    logits = (jnp.dot(xh, wrh_ref[...], preferred_element_type=f32)
              + jnp.dot(xh, wrl_ref[...], preferred_element_type=f32)
              + jnp.dot(xl, wrh_ref[...], preferred_element_type=f32)) + br_ref[...]
    lane = lax.broadcasted_iota(jnp.int32, (tm, LANES), 1)
    lane_f = lane.astype(f32)
    work = jnp.where(lane < N_EXPERTS, logits, -jnp.inf)

    vals, idxs, hots = [], [], []
    for _ in range(TOP_K):
        mx = jnp.max(work, axis=1, keepdims=True)
        idx = jnp.min(jnp.where(work == mx, lane_f, float(LANES)), axis=1, keepdims=True)
        hot = lane_f == idx
        work = jnp.where(hot, -jnp.inf, work)
        vals.append(mx)
        idxs.append(idx)
        hots.append(hot)

    exps = [jnp.exp(v - vals[0]) for v in vals]
    denom = exps[0] + exps[1] + exps[2] + exps[3]
    gates = [e / denom for e in exps]

    sel = jnp.zeros((tm, LANES), f32)
    for hot in hots:
        sel = sel + hot.astype(f32)
    before = jnp.dot(tri_ref[...], sel.astype(bf16), preferred_element_type=f32) + carry_ref[...]
    ranks = [jnp.sum(jnp.where(hot, before, 0.0), axis=1, keepdims=True) for hot in hots]
    carry = carry_ref[...] + jnp.sum(sel, axis=0, keepdims=True)
    carry_ref[...] = carry
    cnt_ref[...] = jnp.broadcast_to(carry, cnt_ref.shape)

    route = jnp.zeros((tm, LANES), f32)
    for base, cols in ((ROUTE_IDX, idxs), (ROUTE_GATE, gates), (ROUTE_RANK, ranks)):
        for k, colv in enumerate(cols):
            route = jnp.where(lane == base + k, colv, route)
    route_ref[...] = route


def oproj_router(o, h, wo, g, wr_hi, wr_lo, br):
    n, d = h.shape
    tm = min(TOKEN_BLOCK, n)
    tri = (jnp.arange(tm)[None, :] < jnp.arange(tm)[:, None]).astype(bf16)
    row = lambda i: (i, 0)
    fixed = lambda i: (0, 0)
    return pl.pallas_call(
        _oproj_router_kernel,
        grid=(n // tm,),
        in_specs=[pl.BlockSpec((tm, d), row), pl.BlockSpec((tm, d), row),
                  pl.BlockSpec((d, d), fixed), pl.BlockSpec((1, d), fixed),
                  pl.BlockSpec((d, LANES), fixed), pl.BlockSpec((d, LANES), fixed),
                  pl.BlockSpec((1, LANES), fixed), pl.BlockSpec((tm, tm), fixed)],
        out_specs=[pl.BlockSpec((tm, d), row), pl.BlockSpec((tm, d), row),
                   pl.BlockSpec((tm, LANES), row), pl.BlockSpec((8, LANES), fixed)],
        out_shape=[jax.ShapeDtypeStruct((n, d), f32), jax.ShapeDtypeStruct((n, d), f32),
                   jax.ShapeDtypeStruct((n, LANES), f32), jax.ShapeDtypeStruct((8, LANES), f32)],
        scratch_shapes=[pltpu.VMEM((1, LANES), f32)],
        compiler_params=_params("arbitrary"), name="oproj_router",
    )(o, h, wo, g.reshape(1, d), wr_hi, wr_lo, br, tri)


def _dispatch_kernel(zlo_ref, zhi_ref, *refs):
    pos_refs = refs[:TOP_K]
    x_ref, rows_ref, zero_ref, sem = refs[TOP_K:]
    tm = x_ref.shape[0]

    def row_copy(src, r):
        return pltpu.make_async_copy(src, rows_ref.at[pl.ds(r, 1)], sem)

    @pl.when(pl.program_id(0) == 0)
    def _():
        zero_ref[...] = jnp.zeros_like(zero_ref)
        zrows = zero_ref.shape[0]

        def chunk(start, size, wait):
            if size > 1:
                start = pl.multiple_of(start, SUBLANES)
            cp = pltpu.make_async_copy(zero_ref.at[pl.ds(0, size)], rows_ref.at[pl.ds(start, size)], sem)
            cp.wait() if wait else cp.start()

        for wait in (False, True):
            def per_expert(e, _, wait=wait):
                hi = zhi_ref[e]
                aligned = jnp.minimum(hi, (zlo_ref[e] + SUBLANES - 1) // SUBLANES * SUBLANES)

                def single(r, _):
                    chunk(r, 1, wait)
                    return 0
                lax.fori_loop(zlo_ref[e], aligned, single, 0)
                length = hi - aligned
                n_whole = length // zrows

                def whole(j, _):
                    chunk(aligned + j * zrows, zrows, wait)
                    return 0
                lax.fori_loop(0, n_whole, whole, 0)
                start = aligned + n_whole * zrows
                size = zrows // 2
                while size >= SUBLANES:
                    has = (length & size) != 0

                    @pl.when(has)
                    def _(start=start, size=size):
                        chunk(start, size, wait)
                    start = start + jnp.where(has, size, 0)
                    size //= 2
                return 0
            lax.fori_loop(0, N_EXPERTS, per_expert, 0)

    for wait in (False, True):
        def per_token(t, _, wait=wait):
            for k in range(TOP_K):
                cp = row_copy(x_ref.at[pl.ds(t, 1)], pos_refs[k][t])
                cp.wait() if wait else cp.start(priority=k % 2)
            return 0
        lax.fori_loop(0, tm, per_token, 0, unroll=DMA_LOOP_UNROLL)


def dispatch(xn, pos_flat, zero_lo, zero_hi, n_rows):
    n, d = xn.shape
    tm = min(SCATTER_BLOCK, n)
    steps = n // tm
    slot_specs = [pl.BlockSpec((tm,), lambda i, *_, k=k: (k * steps + i,), memory_space=pltpu.SMEM)
                  for k in range(TOP_K)]
    grid_spec = pltpu.PrefetchScalarGridSpec(
        num_scalar_prefetch=2, grid=(steps,),
        in_specs=slot_specs + [pl.BlockSpec((tm, d), lambda i, *_: (i, 0))],
        out_specs=pl.BlockSpec(memory_space=pl.ANY),
        scratch_shapes=[pltpu.VMEM((ROW_BLOCK, d), f32), pltpu.SemaphoreType.DMA(())])
    return pl.pallas_call(
        _dispatch_kernel, grid_spec=grid_spec,
        out_shape=jax.ShapeDtypeStruct((n_rows, d), f32),
        compiler_params=_params("arbitrary"), name="dispatch",
    )(zero_lo, zero_hi, *([pos_flat] * TOP_K), xn)


def _experts_kernel(be_ref, nused_ref, x_ref, wgu_ref, bgu_ref, wd_ref, bd_ref, y_ref, wgu_b, wd_b):
    rb = pl.program_id(0)
    d_ff = wd_ref.shape[2]
    used = rb < nused_ref[0]
    new_expert = (rb == 0) | (be_ref[rb] != be_ref[jnp.maximum(rb - 1, 0)])

    @pl.when(used & new_expert)
    def _():
        wgu_b[...] = wgu_ref[0, 0].astype(bf16)
        wd_b[...] = wd_ref[0, 0].astype(bf16)

    @pl.when(used)
    def _():
        x = x_ref[...].astype(bf16)
        hgu = jnp.dot(x, wgu_b[...], preferred_element_type=f32) + bgu_ref[0, 0]
        glu = jnp.minimum(hgu[:, :d_ff], SWIGLU_LIMIT)
        lin = jnp.clip(hgu[:, d_ff:], -SWIGLU_LIMIT, SWIGLU_LIMIT)
        act = glu * (1.0 / (1.0 + jnp.exp(-SWIGLU_ALPHA * glu))) * (lin + 1.0)
        y_ref[...] = jnp.dot(act.astype(bf16), wd_b[...], preferred_element_type=f32) + bd_ref[0, 0]

    @pl.when(jnp.logical_not(used))
    def _():
        y_ref[...] = jnp.zeros_like(y_ref)


def experts(x_rows, block_expert, n_used, layer, w_gu, b_gu, w_down, b_down):
    n_rows, d = x_rows.shape
    f2 = w_gu.shape[3]
    d_ff = f2 // 2
    wmap = lambda rb, be, nu: (layer, be[rb], 0, 0)
    rows = pl.BlockSpec((ROW_BLOCK, d), lambda rb, be, nu: (rb, 0))
    grid_spec = pltpu.PrefetchScalarGridSpec(
        num_scalar_prefetch=2, grid=(n_rows // ROW_BLOCK,),
        in_specs=[rows, pl.BlockSpec((1, 1, d, f2), wmap), pl.BlockSpec((1, 1, 1, f2), wmap),
                  pl.BlockSpec((1, 1, d_ff, d), wmap), pl.BlockSpec((1, 1, 1, d), wmap)],
        out_specs=rows,
        scratch_shapes=[pltpu.VMEM((d, f2), bf16), pltpu.VMEM((d_ff, d), bf16)])
    return pl.pallas_call(
        _experts_kernel, grid_spec=grid_spec,
        out_shape=jax.ShapeDtypeStruct((n_rows, d), f32),
        compiler_params=pltpu.CompilerParams(dimension_semantics=("arbitrary",),
                                             vmem_limit_bytes=EXPERTS_VMEM_LIMIT_BYTES),
        name="experts",
    )(block_expert, n_used, x_rows, w_gu, b_gu, w_down, b_down)


def _combine_ple_kernel(*refs, final):
    pos_ref, pos_next_ref = refs[:TOP_K], refs[TOP_K:2 * TOP_K]
    (route_ref, h_ref, p_ref, g_ref, wg_ref, wp_ref, gf_ref,
     yrows_ref, o_ref, ybuf0_ref, ybuf1_ref, sems) = refs[2 * TOP_K:]
    i = pl.program_id(0)
    tm = h_ref.shape[0]

    def row_copy(idx_ref, buf, s, t, k):
        r = 0 if idx_ref is None else idx_ref[k][t]
        return pltpu.make_async_copy(yrows_ref.at[pl.ds(r, 1)], buf.at[k, pl.ds(t, 1)], sems.at[s])

    def start_rows(idx_ref, buf, s, tokens):
        for t in tokens:
            for k in range(TOP_K):
                row_copy(idx_ref, buf, s, t, k).start(priority=k % 2)

    def wait_rows(buf, s):
        def per_token(t, _):
            for k in range(TOP_K):
                row_copy(None, buf, s, t, k).wait()
            return 0
        lax.fori_loop(0, tm, per_token, 0, unroll=DMA_LOOP_UNROLL)

    @pl.when(i == 0)
    def _():
        def per_token(t, _):
            start_rows(pos_ref, ybuf0_ref, 0, [t])
            return 0
        lax.fori_loop(0, tm, per_token, 0, unroll=DMA_LOOP_UNROLL)

    def step(cur, nxt, s_cur, s_nxt):
        wait_rows(cur, s_cur)
        start_rows(pos_next_ref, nxt, s_nxt, range(tm))
        proj = jnp.dot(p_ref[0].astype(bf16), wp_ref[...], preferred_element_type=f32)
        route = route_ref[...]
        h2 = h_ref[...]
        for k in range(TOP_K):
            h2 = h2 + route[:, ROUTE_GATE + k:ROUTE_GATE + k + 1] * cur[k]
        hn = _rmsnorm(h2, g_ref[...]).astype(bf16)
        gate = 1.0 / (1.0 + jnp.exp(-jnp.dot(hn, wg_ref[...], preferred_element_type=f32)))
        h3 = h2 + gate * proj
        if final:
            h3 = _rmsnorm(h3, gf_ref[...])
        o_ref[...] = h3

        @pl.when(i == pl.num_programs(0) - 1)
        def _():
            wait_rows(nxt, s_nxt)

    @pl.when(i % 2 == 0)
    def _():
        step(ybuf0_ref, ybuf1_ref, 0, 1)

    @pl.when(i % 2 == 1)
    def _():
        step(ybuf1_ref, ybuf0_ref, 1, 0)


def combine_ple(pos_flat, route, h1, y_rows, p, layer, g, wg, wp, g_final, final):
    n, d = h1.shape
    pd = p.shape[2]
    tm = min(GATHER_BLOCK, n)
    steps = n // tm
    row = lambda i: (i, 0)
    fixed = lambda i: (0, 0)
    slots = [pl.BlockSpec((tm,), lambda i, k=k: (k * steps + i,), memory_space=pltpu.SMEM)
             for k in range(TOP_K)]
    slots_next = [pl.BlockSpec((tm,), lambda i, k=k: (k * steps + jnp.minimum(i + 1, steps - 1),),
                               memory_space=pltpu.SMEM) for k in range(TOP_K)]
    return pl.pallas_call(
        functools.partial(_combine_ple_kernel, final=final),
        grid=(steps,),
        in_specs=slots + slots_next + [
                  pl.BlockSpec((tm, LANES), row), pl.BlockSpec((tm, d), row),
                  pl.BlockSpec((1, tm, pd), lambda i: (layer, i, 0)), pl.BlockSpec((1, d), fixed),
                  pl.BlockSpec((d, d), fixed), pl.BlockSpec((pd, d), fixed),
                  pl.BlockSpec((1, d), fixed), pl.BlockSpec(memory_space=pl.ANY)],
        out_specs=pl.BlockSpec((tm, d), row),
        out_shape=jax.ShapeDtypeStruct((n, d), f32),
        scratch_shapes=[pltpu.VMEM((TOP_K, tm, d), f32), pltpu.VMEM((TOP_K, tm, d), f32),
                        pltpu.SemaphoreType.DMA((2,))],
        compiler_params=_params("arbitrary"), name="combine_ple",
    )(*([pos_flat] * (2 * TOP_K)), route, h1, p, g.reshape(1, d), wg, wp, g_final.reshape(1, d), y_rows)


def _routing_tables(route, counts_f, n):
    counts = counts_f[0, :N_EXPERTS].astype(jnp.int32)
    padded = (counts + ROW_BLOCK - 1) // ROW_BLOCK * ROW_BLOCK
    pad_end = jnp.cumsum(padded)
    pad_start = pad_end - padded
    n_blocks = -(-(n * TOP_K + N_EXPERTS * (ROW_BLOCK - 1)) // ROW_BLOCK)
    n_rows = n_blocks * ROW_BLOCK
    route_t = route[:, :ROUTE_RANK + TOP_K].T
    idx = route_t[ROUTE_IDX:ROUTE_IDX + TOP_K].astype(jnp.int32)
    rank = route_t[ROUTE_RANK:ROUTE_RANK + TOP_K].astype(jnp.int32)
    pos_flat = (pad_start[idx] + rank).reshape(-1)
    block_start = jnp.arange(n_blocks, dtype=jnp.int32) * ROW_BLOCK
    block_expert = jnp.minimum(jnp.sum(block_start[:, None] >= pad_end[None, :], axis=1),
                               N_EXPERTS - 1).astype(jnp.int32)
    n_used = (pad_end[-1:] // ROW_BLOCK).astype(jnp.int32)
    zero_lo = (pad_start + counts).astype(jnp.int32)
    zero_hi = jnp.concatenate([pad_start[1:], jnp.array([n_rows], jnp.int32)]).astype(jnp.int32)
    return pos_flat, block_expert, n_used, zero_lo, zero_hi, n_rows


def kernel(x, p, g_mix, w_qkv, w_fgate, b_fgate, w_o, g_moe, w_router, b_router, w_gu, b_gu,
           w_down, b_down, g_ple, w_ple_gate, w_ple_proj, g_final):
    batch, seq, d = x.shape
    depth = w_qkv.shape[0]
    n = batch * seq
    h = x.reshape(n, d)
    q_scale = jnp.concatenate([jnp.full((d,), HEAD_DIM ** -0.5, f32), jnp.ones((2 * d,), f32)])
    n_exp = w_gu.shape[1]
    b_gu4 = b_gu.reshape(depth, n_exp, 1, -1)
    b_down4 = b_down.reshape(depth, n_exp, 1, -1)
    for i in range(depth):
        wqkv = (w_qkv[i] * q_scale).astype(bf16)
        if i % 2 == 0:
            qkv = norm_proj(h, g_mix[i], wqkv)
            o = attention(qkv, batch, seq)
        else:
            j = i // 2
            wf = jnp.pad(w_fgate[j], ((0, 0), (0, LANES - N_HEADS))).astype(bf16)
            bf = jnp.pad(b_fgate[j], (0, LANES - N_HEADS)).reshape(1, LANES)
            qkv, f_logit = norm_proj(h, g_mix[i], wqkv, wf)
            c = fox_cumsum(f_logit, bf, batch, seq)
            o = attention(qkv, batch, seq, c)

        wr = jnp.pad(w_router[i], ((0, 0), (0, LANES - N_EXPERTS)))
        wr_hi = wr.astype(bf16)
        wr_lo = (wr - wr_hi.astype(f32)).astype(bf16)
        br = jnp.pad(b_router[i], (0, LANES - N_EXPERTS)).reshape(1, LANES)
        h1, xn, route, counts = oproj_router(o, h, w_o[i].astype(bf16), g_moe[i], wr_hi, wr_lo, br)

        pos_flat, block_expert, n_used, zero_lo, zero_hi, n_rows = _routing_tables(route, counts, n)
        x_rows = dispatch(xn, pos_flat, zero_lo, zero_hi, n_rows)
        y_rows = experts(x_rows, block_expert, n_used, i, w_gu, b_gu4, w_down, b_down4)
        h = combine_ple(pos_flat, route, h1, y_rows, p.reshape(depth, n, -1), i, g_ple[i],
                        w_ple_gate[i].astype(bf16), w_ple_proj[i].astype(bf16), g_final,
                        final=(i == depth - 1))
    return h.reshape(batch, seq, d)
```

```python
import functools

import jax
import jax.numpy as jnp
from jax import lax
from jax.experimental import pallas as pl
from jax.experimental.pallas import tpu as pltpu

N_HEADS = 16
HEAD_DIM = 64
N_EXPERTS = 32
TOP_K = 4
SWIGLU_LIMIT = 7.0
SWIGLU_ALPHA = 1.702
RMS_EPS = 1e-6

LANES = 128
SUBLANES = 8
HEADS_PER_STEP = LANES // HEAD_DIM
VMEM_LIMIT_BYTES = 48 * 1024 * 1024
EXPERTS_VMEM_LIMIT_BYTES = 56 * 1024 * 1024

ROW_BLOCK = 512
ATTN_BLOCK = 256
TOKEN_BLOCK = 512
GATHER_BLOCK = 256
SCATTER_BLOCK = 512

SB_SKIP_BELOW = -104.0
DMA_LOOP_UNROLL = 8
ATTN_LANE_GROUPS = 2
FOX_KEY_BLOCK = 2048

ROUTE_IDX, ROUTE_GATE, ROUTE_RANK = 0, TOP_K, 2 * TOP_K

f32 = jnp.float32
bf16 = jnp.bfloat16


def _params(*semantics):
    return pltpu.CompilerParams(dimension_semantics=semantics, vmem_limit_bytes=VMEM_LIMIT_BYTES)


def _rmsnorm(x, g):
    var = jnp.mean(x * x, axis=-1, keepdims=True)
    return x * lax.rsqrt(var + RMS_EPS) * g


def _neg_abs(x):
    bits = lax.bitcast_convert_type(x, jnp.uint32) | jnp.uint32(0x80000000)
    return lax.bitcast_convert_type(bits, f32)


def _split2(x):
    hi = x.astype(bf16)
    lo = (x - hi.astype(f32)).astype(bf16)
    return hi, lo


def _norm_proj_kernel(h_ref, g_ref, w_ref, *rest, with_gate):
    xn = _rmsnorm(h_ref[...], g_ref[...]).astype(bf16)
    if with_gate:
        wf_ref, o_ref, of_ref = rest
        of_ref[...] = jnp.dot(xn, wf_ref[...], preferred_element_type=f32)
    else:
        (o_ref,) = rest
    o_ref[...] = jnp.dot(xn, w_ref[...], preferred_element_type=f32).astype(o_ref.dtype)


def norm_proj(h, g, w, wf=None):
    n, d = h.shape
    m = w.shape[1]
    tm = min(TOKEN_BLOCK, n)
    with_gate = wf is not None
    in_specs = [pl.BlockSpec((tm, d), lambda i: (i, 0)),
                pl.BlockSpec((1, d), lambda i: (0, 0)),
                pl.BlockSpec((d, m), lambda i: (0, 0))]
    out_specs = [pl.BlockSpec((tm, m), lambda i: (i, 0))]
    out_shape = [jax.ShapeDtypeStruct((n, m), bf16)]
    args = [h, g.reshape(1, d), w]
    if with_gate:
        in_specs.append(pl.BlockSpec((d, LANES), lambda i: (0, 0)))
        out_specs.append(pl.BlockSpec((tm, LANES), lambda i: (i, 0)))
        out_shape.append(jax.ShapeDtypeStruct((n, LANES), f32))
        args.append(wf)
    out = pl.pallas_call(
        functools.partial(_norm_proj_kernel, with_gate=with_gate),
        grid=(n // tm,), in_specs=in_specs, out_specs=out_specs, out_shape=out_shape,
        compiler_params=_params("parallel"), name="norm_proj")(*args)
    return out if with_gate else out[0]


def _fox_cumsum_kernel(f_ref, b_ref, tri_ref, c_ref, carry_ref):
    @pl.when(pl.program_id(1) == 0)
    def _():
        carry_ref[...] = jnp.zeros_like(carry_ref)

    x = f_ref[0] + b_ref[...]
    log_f = jnp.minimum(x, 0.0) - jnp.log(1.0 + jnp.exp(-jnp.abs(x)))
    p1 = log_f.astype(bf16)
    r1 = log_f - p1.astype(f32)
    p2 = r1.astype(bf16)
    p3 = (r1 - p2.astype(f32)).astype(bf16)
    tri = tri_ref[...]
    c = (jnp.dot(tri, p1, preferred_element_type=f32)
         + jnp.dot(tri, p2, preferred_element_type=f32)
         + jnp.dot(tri, p3, preferred_element_type=f32)) + carry_ref[...]
    c_ref[0] = c
    carry_ref[...] = c[-1:, :]


def fox_cumsum(f_logit, b_pad, batch, seq):
    ts = min(ATTN_BLOCK, seq)
    tri = (jnp.arange(ts)[None, :] <= jnp.arange(ts)[:, None]).astype(bf16)
    return pl.pallas_call(
        _fox_cumsum_kernel,
        grid=(batch, seq // ts),
        in_specs=[pl.BlockSpec((1, ts, LANES), lambda b, s: (b, s, 0)),
                  pl.BlockSpec((1, LANES), lambda b, s: (0, 0)),
                  pl.BlockSpec((ts, ts), lambda b, s: (0, 0))],
        out_specs=pl.BlockSpec((1, ts, LANES), lambda b, s: (b, s, 0)),
        out_shape=jax.ShapeDtypeStruct((batch, seq, LANES), f32),
        scratch_shapes=[pltpu.VMEM((1, LANES), f32)],
        compiler_params=_params("parallel", "arbitrary"), name="fox_cumsum",
    )(f_logit.reshape(batch, seq, LANES), b_pad, tri)


def _head_mask(hh):
    lane = lax.broadcasted_iota(jnp.int32, (1, LANES), 1)
    return (lane >= hh * HEAD_DIM) & (lane < (hh + 1) * HEAD_DIM)


def _qk(qm, k2):
    return lax.dot_general(qm, k2, (((1,), (1,)), ((), ())), preferred_element_type=f32)


def _sb_attn_kernel(q_ref, k_ref, v_ref, u_ref, o_ref):
    qi = pl.program_id(2)
    tq = q_ref.shape[1]
    tk = tq
    groups = q_ref.shape[2] // LANES
    u2 = u_ref[...]
    masks = [_head_mask(hh) for hh in range(HEADS_PER_STEP)]
    neg_qs = []
    for g in range(groups):
        q2 = q_ref[0, :, g * LANES:(g + 1) * LANES]
        neg_qs += [jnp.where(m, -q2, jnp.zeros_like(q2)) for m in masks]
    n_heads = len(neg_qs)

    def two_blocks(kb, runs, accs, diag):
        has_next = kb >= 1
        kbs = [kb, jnp.maximum(kb - 1, 0)]
        if diag:
            strict = (lax.broadcasted_iota(jnp.int32, (tq, tk), 1)
                      < lax.broadcasted_iota(jnp.int32, (tq, tk), 0))
        chains = [(blk, h) for blk in range(2) for h in range(n_heads)]

        def keys(ref, blk, h):
            g = h // HEADS_PER_STEP
            return ref[0, pl.ds(kbs[blk] * tk, tk), g * LANES:(g + 1) * LANES]

        nzs = {c: _qk(neg_qs[c[1]], keys(k_ref, *c)) for c in chains}
        suffixes = {}
        for c in chains:
            nz = nzs[c]
            log_not = jnp.minimum(nz, 0.0) - jnp.log(1.0 + jnp.exp(_neg_abs(nz)))
            if diag and c[0] == 0:
                log_not = jnp.where(strict, log_not, 0.0)
            hi, lo = _split2(log_not)
            suffixes[c] = jnp.dot(jnp.concatenate([hi, lo], axis=1), u2, preferred_element_type=f32)
        weights, new_runs = {}, []
        for h in range(n_heads):
            after = runs[h] + suffixes[(0, h)][:, 0:1]
            new_runs.append(jnp.where(has_next, after + suffixes[(1, h)][:, 0:1], after))
            for blk, run in ((0, runs[h]), (1, jnp.where(has_next, after, -jnp.inf))):
                a = jnp.exp((run - nzs[(blk, h)]) + suffixes[(blk, h)])
                if diag and blk == 0:
                    a = jnp.where(strict, a, 0.0)
                weights[(blk, h)] = a.astype(bf16)
        new_accs = []
        for g in range(groups):
            hs = range(g * HEADS_PER_STEP, (g + 1) * HEADS_PER_STEP)
            v2s = [keys(v_ref, blk, g * HEADS_PER_STEP) for blk in range(2)]
            vv = jnp.concatenate([jnp.where(m, v2, jnp.zeros_like(v2)) for v2 in v2s for m in masks], axis=0)
            ww = jnp.concatenate([weights[(blk, h)] for blk in range(2) for h in hs], axis=1)
            new_accs.append(accs[g] + jnp.dot(ww, vv, preferred_element_type=f32))
        return new_runs, new_accs

    def alive(runs):
        top = functools.reduce(jnp.maximum, runs)
        return (jnp.max(top) > SB_SKIP_BELOW).astype(jnp.int32)

    zero_run = jnp.zeros((tq, 1), f32)
    runs, accs = two_blocks(qi, [zero_run] * n_heads, [jnp.zeros((tq, LANES), f32)] * groups, True)

    def cond(c):
        return (c[0] >= 0) & (c[1] > 0)

    def body(c):
        runs, accs = two_blocks(c[0], list(c[2]), list(c[3]), False)
        return c[0] - 2, alive(runs), tuple(runs), tuple(accs)

    accs = lax.while_loop(cond, body, (qi - 2, alive(runs), tuple(runs), tuple(accs)))[3]
    o_ref[0] = jnp.concatenate(list(accs), axis=1).astype(o_ref.dtype)


def _fox_attn_kernel(q_ref, k_ref, v_ref, cq_ref, ck_ref, o_ref):
    hp = pl.program_id(1)
    qi = pl.program_id(2)
    tq = q_ref.shape[1]
    tk = min(FOX_KEY_BLOCK, k_ref.shape[1])
    groups = q_ref.shape[2] // LANES
    masks = [_head_mask(hh) for hh in range(HEADS_PER_STEP)]
    q0 = qi * tq
    n_full = q0 // tk
    lane = lax.broadcasted_iota(jnp.int32, (1, LANES), 1)
    c_blk = cq_ref[0]
    qms, cq_ps = [], []
    for g in range(groups):
        q2 = q_ref[0, :, g * LANES:(g + 1) * LANES]
        qms.append([jnp.where(m, q2, jnp.zeros_like(q2)) for m in masks])
        head0 = (hp * groups + g) * HEADS_PER_STEP
        cqs = [jnp.sum(jnp.where(lane == head0 + hh, c_blk, 0.0), axis=1, keepdims=True)
               for hh in range(HEADS_PER_STEP)]
        cq_ps.append(jnp.where(masks[0], cqs[0], cqs[1]))

    def step(k_start, width, masked, carry):
        k_start = pl.multiple_of(k_start, LANES)
        if masked:
            own = (lax.broadcasted_iota(jnp.int32, (tq, tq), 1) <= lax.broadcasted_iota(jnp.int32, (tq, tq), 0))

        def causal(w):
            own_w = jnp.where(own, w[:, width - tq:], -jnp.inf)
            return own_w if width == tq else jnp.concatenate([w[:, :width - tq], own_w], axis=1)
        ws = []
        for g in range(groups):
            k2 = k_ref[0, pl.ds(k_start, width), g * LANES:(g + 1) * LANES]
            for hh in range(HEADS_PER_STEP):
                head = (hp * groups + g) * HEADS_PER_STEP + hh
                ck = ck_ref[0, pl.ds(head, 1), pl.ds(k_start, width)]
                w = _qk(qms[g][hh], k2) - ck
                ws.append(causal(w) if masked else w)
        out = []
        for g in range(groups):
            m_p, l_p, acc = carry[g]
            w0, w1 = ws[g * HEADS_PER_STEP], ws[g * HEADS_PER_STEP + 1]
            mx_p = jnp.where(masks[0], jnp.max(w0, axis=1, keepdims=True), jnp.max(w1, axis=1, keepdims=True))
            m_new = jnp.maximum(m_p, cq_ps[g] + mx_p)
            shift = m_new - cq_ps[g]
            alpha = jnp.exp(m_p - m_new)
            p0 = jnp.exp(w0 - shift[:, 0:1])
            p1 = jnp.exp(w1 - shift[:, HEAD_DIM:HEAD_DIM + 1])
            sum_p = jnp.where(masks[0], jnp.sum(p0, axis=1, keepdims=True), jnp.sum(p1, axis=1, keepdims=True))
            v2 = v_ref[0, pl.ds(k_start, width), g * LANES:(g + 1) * LANES]
            vv = jnp.concatenate([jnp.where(m, v2, jnp.zeros_like(v2)) for m in masks], axis=0)
            acc = acc * alpha + jnp.dot(jnp.concatenate([p0.astype(bf16), p1.astype(bf16)], axis=1), vv,
                                        preferred_element_type=f32)
            out.append((m_new, alpha * l_p + sum_p, acc))
        return tuple(out)

    init = tuple((jnp.full((tq, LANES), -jnp.inf, f32), jnp.zeros((tq, LANES), f32),
                  jnp.zeros((tq, LANES), f32)) for _ in range(groups))
    c = init if tk == k_ref.shape[1] else lax.fori_loop(0, n_full, lambda kb, c: step(kb * tk, tk, False, c), init)
    variants = [functools.partial(step, n_full * tk, (j + 1) * tq, True) for j in range(tk // tq)]
    c = variants[0](c) if len(variants) == 1 else lax.switch((q0 - n_full * tk) // tq, variants, c)
    o_ref[0] = jnp.concatenate([acc / l_p for _, l_p, acc in c], axis=1).astype(o_ref.dtype)


def attention(qkv, batch, seq, c=None):
    d = qkv.shape[1] // 3
    width = LANES * ATTN_LANE_GROUPS
    nlb = d // width
    tq = min(ATTN_BLOCK, seq)
    qkv3 = qkv.reshape(batch, seq, 3 * d)
    in_specs = [pl.BlockSpec((1, tq, width), lambda b, hp, qi: (b, qi, hp)),
                pl.BlockSpec((1, seq, width), lambda b, hp, qi: (b, 0, nlb + hp)),
                pl.BlockSpec((1, seq, width), lambda b, hp, qi: (b, 0, 2 * nlb + hp))]
    args = [qkv3, qkv3, qkv3]
    if c is None:
        kern = _sb_attn_kernel
        u = (jnp.arange(tq)[:, None] >= jnp.arange(tq)[None, :]).astype(bf16)
        in_specs.append(pl.BlockSpec((2 * tq, tq), lambda b, hp, qi: (0, 0)))
        args.append(jnp.concatenate([u, u], axis=0))
    else:
        kern = _fox_attn_kernel
        tk = min(FOX_KEY_BLOCK, seq)
        assert tk % tq == 0 and seq % tk == 0
        ch = jnp.transpose(c[:, :, :N_HEADS], (0, 2, 1))
        in_specs.append(pl.BlockSpec((1, tq, LANES), lambda b, hp, qi: (b, qi, 0)))
        in_specs.append(pl.BlockSpec((1, N_HEADS, seq), lambda b, hp, qi: (b, 0, 0)))
        args += [c, ch]
    out = pl.pallas_call(
        kern, grid=(batch, nlb, seq // tq), in_specs=in_specs,
        out_specs=pl.BlockSpec((1, tq, width), lambda b, hp, qi: (b, qi, hp)),
        out_shape=jax.ShapeDtypeStruct((batch, seq, d), bf16),
        compiler_params=_params("parallel", "parallel", "arbitrary"), name="attention")(*args)
    return out.reshape(batch * seq, d)


def _oproj_router_kernel(o_ref, h_ref, wo_ref, g_ref, wrh_ref, wrl_ref, br_ref, tri_ref,
                         h1_ref, xn_ref, route_ref, cnt_ref, carry_ref):
    @pl.when(pl.program_id(0) == 0)
    def _():
        carry_ref[...] = jnp.zeros_like(carry_ref)

    tm = h_ref.shape[0]
    h1 = h_ref[...] + jnp.dot(o_ref[...], wo_ref[...], preferred_element_type=f32)
    h1_ref[...] = h1
    xn = _rmsnorm(h1, g_ref[...])
    xn_ref[...] = xn

    xh, xl = _split2(xn)
    logits = (jnp.dot(xh, wrh_ref[...], preferred_element_type=f32)
              + jnp.dot(xh, wrl_ref[...], preferred_element_type=f32)
              + jnp.dot(xl, wrh_ref[...], preferred_element_type=f32)) + br_ref[...]
    lane = lax.broadcasted_iota(jnp.int32, (tm, LANES), 1)
    lane_f = lane.astype(f32)
    work = jnp.where(lane < N_EXPERTS, logits, -jnp.inf)

    vals, idxs, hots = [], [], []
    for _ in range(TOP_K):
        mx = jnp.max(work, axis=1, keepdims=True)
        idx = jnp.min(jnp.where(work == mx, lane_f, float(LANES)), axis=1, keepdims=True)
        hot = lane_f == idx
        work = jnp.where(hot, -jnp.inf, work)
        vals.append(mx)
        idxs.append(idx)
        hots.append(hot)

    exps = [jnp.exp(v - vals[0]) for v in vals]
    denom = exps[0] + exps[1] + exps[2] + exps[3]
    gates = [e / denom for e in exps]

    sel = jnp.zeros((tm, LANES), f32)
    for hot in hots:
        sel = sel + hot.astype(f32)
    before = jnp.dot(tri_ref[...], sel.astype(bf16), preferred_element_type=f32) + carry_ref[...]
    ranks = [jnp.sum(jnp.where(hot, before, 0.0), axis=1, keepdims=True) for hot in hots]
    carry = carry_ref[...] + jnp.sum(sel, axis=0, keepdims=True)
    carry_ref[...] = carry
    cnt_ref[...] = jnp.broadcast_to(carry, cnt_ref.shape)

    route = jnp.zeros((tm, LANES), f32)
    for base, cols in ((ROUTE_IDX, idxs), (ROUTE_GATE, gates), (ROUTE_RANK, ranks)):
        for k, colv in enumerate(cols):
            route = jnp.where(lane == base + k, colv, route)
    route_ref[...] = route


def oproj_router(o, h, wo, g, wr_hi, wr_lo, br):
    n, d = h.shape
    tm = min(TOKEN_BLOCK, n)
    tri = (jnp.arange(tm)[None, :] < jnp.arange(tm)[:, None]).astype(bf16)
    row = lambda i: (i, 0)
    fixed = lambda i: (0, 0)
    return pl.pallas_call(
        _oproj_router_kernel,
        grid=(n // tm,),
        in_specs=[pl.BlockSpec((tm, d), row), pl.BlockSpec((tm, d), row),
                  pl.BlockSpec((d, d), fixed), pl.BlockSpec((1, d), fixed),
                  pl.BlockSpec((d, LANES), fixed), pl.BlockSpec((d, LANES), fixed),
                  pl.BlockSpec((1, LANES), fixed), pl.BlockSpec((tm, tm), fixed)],
        out_specs=[pl.BlockSpec((tm, d), row), pl.BlockSpec((tm, d), row),
                   pl.BlockSpec((tm, LANES), row), pl.BlockSpec((8, LANES), fixed)],
        out_shape=[jax.ShapeDtypeStruct((n, d), f32), jax.ShapeDtypeStruct((n, d), f32),
                   jax.ShapeDtypeStruct((n, LANES), f32), jax.ShapeDtypeStruct((8, LANES), f32)],
        scratch_shapes=[pltpu.VMEM((1, LANES), f32)],
        compiler_params=_params("arbitrary"), name="oproj_router",
    )(o, h, wo, g.reshape(1, d), wr_hi, wr_lo, br, tri)


def _dispatch_kernel(zlo_ref, zhi_ref, *refs):
    pos_refs = refs[:TOP_K]
    x_ref, rows_ref, zero_ref, sem = refs[TOP_K:]
    tm = x_ref.shape[0]

    def row_copy(src, r):
        return pltpu.make_async_copy(src, rows_ref.at[pl.ds(r, 1)], sem)

    @pl.when(pl.program_id(0) == 0)
    def _():
        zero_ref[...] = jnp.zeros_like(zero_ref)
        zrows = zero_ref.shape[0]

        def chunk(start, size, wait):
            if size > 1:
                start = pl.multiple_of(start, SUBLANES)
            cp = pltpu.make_async_copy(zero_ref.at[pl.ds(0, size)], rows_ref.at[pl.ds(start, size)], sem)
            cp.wait() if wait else cp.start()

        for wait in (False, True):
            def per_expert(e, _, wait=wait):
                hi = zhi_ref[e]
                aligned = jnp.minimum(hi, (zlo_ref[e] + SUBLANES - 1) // SUBLANES * SUBLANES)

                def single(r, _):
                    chunk(r, 1, wait)
                    return 0
                lax.fori_loop(zlo_ref[e], aligned, single, 0)
                length = hi - aligned
                n_whole = length // zrows

                def whole(j, _):
                    chunk(aligned + j * zrows, zrows, wait)
                    return 0
                lax.fori_loop(0, n_whole, whole, 0)
                start = aligned + n_whole * zrows
                size = zrows // 2
                while size >= SUBLANES:
                    has = (length & size) != 0

                    @pl.when(has)
                    def _(start=start, size=size):
                        chunk(start, size, wait)
                    start = start + jnp.where(has, size, 0)
                    size //= 2
                return 0
            lax.fori_loop(0, N_EXPERTS, per_expert, 0)

    for wait in (False, True):
        def per_token(t, _, wait=wait):
            for k in range(TOP_K):
                cp = row_copy(x_ref.at[pl.ds(t, 1)], pos_refs[k][t])
                cp.wait() if wait else cp.start(priority=k % 2)
            return 0
        lax.fori_loop(0, tm, per_token, 0, unroll=DMA_LOOP_UNROLL)


def dispatch(xn, pos_flat, zero_lo, zero_hi, n_rows):
    n, d = xn.shape
    tm = min(SCATTER_BLOCK, n)
    steps = n // tm
    slot_specs = [pl.BlockSpec((tm,), lambda i, *_, k=k: (k * steps + i,), memory_space=pltpu.SMEM)
                  for k in range(TOP_K)]
    grid_spec = pltpu.PrefetchScalarGridSpec(
        num_scalar_prefetch=2, grid=(steps,),
        in_specs=slot_specs + [pl.BlockSpec((tm, d), lambda i, *_: (i, 0))],
        out_specs=pl.BlockSpec(memory_space=pl.ANY),
        scratch_shapes=[pltpu.VMEM((ROW_BLOCK, d), f32), pltpu.SemaphoreType.DMA(())])
    return pl.pallas_call(
        _dispatch_kernel, grid_spec=grid_spec,
        out_shape=jax.ShapeDtypeStruct((n_rows, d), f32),
        compiler_params=_params("arbitrary"), name="dispatch",
    )(zero_lo, zero_hi, *([pos_flat] * TOP_K), xn)


def _experts_kernel(be_ref, nused_ref, x_ref, wgu_ref, bgu_ref, wd_ref, bd_ref, y_ref, wgu_b, wd_b):
    rb = pl.program_id(0)
    d_ff = wd_ref.shape[2]
    used = rb < nused_ref[0]
    new_expert = (rb == 0) | (be_ref[rb] != be_ref[jnp.maximum(rb - 1, 0)])

    @pl.when(used & new_expert)
    def _():
        wgu_b[...] = wgu_ref[0, 0].astype(bf16)
        wd_b[...] = wd_ref[0, 0].astype(bf16)

    @pl.when(used)
    def _():
        x = x_ref[...].astype(bf16)
        hgu = jnp.dot(x, wgu_b[...], preferred_element_type=f32) + bgu_ref[0, 0]
        glu = jnp.minimum(hgu[:, :d_ff], SWIGLU_LIMIT)
        lin = jnp.clip(hgu[:, d_ff:], -SWIGLU_LIMIT, SWIGLU_LIMIT)
        act = glu * (1.0 / (1.0 + jnp.exp(-SWIGLU_ALPHA * glu))) * (lin + 1.0)
        y_ref[...] = jnp.dot(act.astype(bf16), wd_b[...], preferred_element_type=f32) + bd_ref[0, 0]

    @pl.when(jnp.logical_not(used))
    def _():
        y_ref[...] = jnp.zeros_like(y_ref)


def experts(x_rows, block_expert, n_used, layer, w_gu, b_gu, w_down, b_down):
    n_rows, d = x_rows.shape
    f2 = w_gu.shape[3]
    d_ff = f2 // 2
    wmap = lambda rb, be, nu: (layer, be[rb], 0, 0)
    rows = pl.BlockSpec((ROW_BLOCK, d), lambda rb, be, nu: (rb, 0))
    grid_spec = pltpu.PrefetchScalarGridSpec(
        num_scalar_prefetch=2, grid=(n_rows // ROW_BLOCK,),
        in_specs=[rows, pl.BlockSpec((1, 1, d, f2), wmap), pl.BlockSpec((1, 1, 1, f2), wmap),
                  pl.BlockSpec((1, 1, d_ff, d), wmap), pl.BlockSpec((1, 1, 1, d), wmap)],
        out_specs=rows,
        scratch_shapes=[pltpu.VMEM((d, f2), bf16), pltpu.VMEM((d_ff, d), bf16)])
    return pl.pallas_call(
        _experts_kernel, grid_spec=grid_spec,
        out_shape=jax.ShapeDtypeStruct((n_rows, d), f32),
        compiler_params=pltpu.CompilerParams(dimension_semantics=("arbitrary",),
                                             vmem_limit_bytes=EXPERTS_VMEM_LIMIT_BYTES),
        name="experts",
    )(block_expert, n_used, x_rows, w_gu, b_gu, w_down, b_down)


def _combine_ple_kernel(*refs, final):
    pos_ref, pos_next_ref = refs[:TOP_K], refs[TOP_K:2 * TOP_K]
    (route_ref, h_ref, p_ref, g_ref, wg_ref, wp_ref, gf_ref,
     yrows_ref, o_ref, ybuf0_ref, ybuf1_ref, sems) = refs[2 * TOP_K:]
    i = pl.program_id(0)
    tm = h_ref.shape[0]

    def row_copy(idx_ref, buf, s, t, k):
        r = 0 if idx_ref is None else idx_ref[k][t]
        return pltpu.make_async_copy(yrows_ref.at[pl.ds(r, 1)], buf.at[k, pl.ds(t, 1)], sems.at[s])

    def start_rows(idx_ref, buf, s, tokens):
        for t in tokens:
            for k in range(TOP_K):
                row_copy(idx_ref, buf, s, t, k).start(priority=k % 2)

    def wait_rows(buf, s):
        def per_token(t, _):
            for k in range(TOP_K):
                row_copy(None, buf, s, t, k).wait()
            return 0
        lax.fori_loop(0, tm, per_token, 0, unroll=DMA_LOOP_UNROLL)

    @pl.when(i == 0)
    def _():
        def per_token(t, _):
            start_rows(pos_ref, ybuf0_ref, 0, [t])
            return 0
        lax.fori_loop(0, tm, per_token, 0, unroll=DMA_LOOP_UNROLL)

    def step(cur, nxt, s_cur, s_nxt):
        wait_rows(cur, s_cur)
        start_rows(pos_next_ref, nxt, s_nxt, range(tm))
        proj = jnp.dot(p_ref[0].astype(bf16), wp_ref[...], preferred_element_type=f32)
        route = route_ref[...]
        h2 = h_ref[...]
        for k in range(TOP_K):
            h2 = h2 + route[:, ROUTE_GATE + k:ROUTE_GATE + k + 1] * cur[k]
        hn = _rmsnorm(h2, g_ref[...]).astype(bf16)
        gate = 1.0 / (1.0 + jnp.exp(-jnp.dot(hn, wg_ref[...], preferred_element_type=f32)))
        h3 = h2 + gate * proj
        if final:
            h3 = _rmsnorm(h3, gf_ref[...])
        o_ref[...] = h3

        @pl.when(i == pl.num_programs(0) - 1)
        def _():
            wait_rows(nxt, s_nxt)

    @pl.when(i % 2 == 0)
    def _():
        step(ybuf0_ref, ybuf1_ref, 0, 1)

    @pl.when(i % 2 == 1)
    def _():
        step(ybuf1_ref, ybuf0_ref, 1, 0)


def combine_ple(pos_flat, route, h1, y_rows, p, layer, g, wg, wp, g_final, final):
    n, d = h1.shape
    pd = p.shape[2]
    tm = min(GATHER_BLOCK, n)
    steps = n // tm
    row = lambda i: (i, 0)
    fixed = lambda i: (0, 0)
    slots = [pl.BlockSpec((tm,), lambda i, k=k: (k * steps + i,), memory_space=pltpu.SMEM)
             for k in range(TOP_K)]
    slots_next = [pl.BlockSpec((tm,), lambda i, k=k: (k * steps + jnp.minimum(i + 1, steps - 1),),
                               memory_space=pltpu.SMEM) for k in range(TOP_K)]
    return pl.pallas_call(
        functools.partial(_combine_ple_kernel, final=final),
        grid=(steps,),
        in_specs=slots + slots_next + [
                  pl.BlockSpec((tm, LANES), row), pl.BlockSpec((tm, d), row),
                  pl.BlockSpec((1, tm, pd), lambda i: (layer, i, 0)), pl.BlockSpec((1, d), fixed),
                  pl.BlockSpec((d, d), fixed), pl.BlockSpec((pd, d), fixed),
                  pl.BlockSpec((1, d), fixed), pl.BlockSpec(memory_space=pl.ANY)],
        out_specs=pl.BlockSpec((tm, d), row),
        out_shape=jax.ShapeDtypeStruct((n, d), f32),
        scratch_shapes=[pltpu.VMEM((TOP_K, tm, d), f32), pltpu.VMEM((TOP_K, tm, d), f32),
                        pltpu.SemaphoreType.DMA((2,))],
        compiler_params=_params("arbitrary"), name="combine_ple",
    )(*([pos_flat] * (2 * TOP_K)), route, h1, p, g.reshape(1, d), wg, wp, g_final.reshape(1, d), y_rows)


def _routing_tables(route, counts_f, n):
    counts = counts_f[0, :N_EXPERTS].astype(jnp.int32)
    padded = (counts + ROW_BLOCK - 1) // ROW_BLOCK * ROW_BLOCK
    pad_end = jnp.cumsum(padded)
    pad_start = pad_end - padded
    n_blocks = -(-(n * TOP_K + N_EXPERTS * (ROW_BLOCK - 1)) // ROW_BLOCK)
    n_rows = n_blocks * ROW_BLOCK
    route_t = route[:, :ROUTE_RANK + TOP_K].T
    idx = route_t[ROUTE_IDX:ROUTE_IDX + TOP_K].astype(jnp.int32)
    rank = route_t[ROUTE_RANK:ROUTE_RANK + TOP_K].astype(jnp.int32)
    start_of = jnp.zeros_like(idx)
    for e in range(N_EXPERTS):
        start_of = jnp.where(idx == e, pad_start[e], start_of)
    pos_flat = (start_of + rank).reshape(-1)
    block_start = jnp.arange(n_blocks, dtype=jnp.int32) * ROW_BLOCK
    block_expert = jnp.minimum(jnp.sum(block_start[:, None] >= pad_end[None, :], axis=1),
                               N_EXPERTS - 1).astype(jnp.int32)
    n_used = (pad_end[-1:] // ROW_BLOCK).astype(jnp.int32)
    zero_lo = (pad_start + counts).astype(jnp.int32)
    zero_hi = jnp.concatenate([pad_start[1:], jnp.array([n_rows], jnp.int32)]).astype(jnp.int32)
    return pos_flat, block_expert, n_used, zero_lo, zero_hi, n_rows


def kernel(x, p, g_mix, w_qkv, w_fgate, b_fgate, w_o, g_moe, w_router, b_router, w_gu, b_gu,
           w_down, b_down, g_ple, w_ple_gate, w_ple_proj, g_final):
    batch, seq, d = x.shape
    depth = w_qkv.shape[0]
    n = batch * seq
    h = x.reshape(n, d)
    q_scale = jnp.concatenate([jnp.full((d,), HEAD_DIM ** -0.5, f32), jnp.ones((2 * d,), f32)])
    n_exp = w_gu.shape[1]
    b_gu4 = b_gu.reshape(depth, n_exp, 1, -1)
    b_down4 = b_down.reshape(depth, n_exp, 1, -1)
    for i in range(depth):
        wqkv = (w_qkv[i] * q_scale).astype(bf16)
        if i % 2 == 0:
            qkv = norm_proj(h, g_mix[i], wqkv)
            o = attention(qkv, batch, seq)
        else:
            j = i // 2
            wf = jnp.pad(w_fgate[j], ((0, 0), (0, LANES - N_HEADS))).astype(bf16)
            bf = jnp.pad(b_fgate[j], (0, LANES - N_HEADS)).reshape(1, LANES)
            qkv, f_logit = norm_proj(h, g_mix[i], wqkv, wf)
            c = fox_cumsum(f_logit, bf, batch, seq)
            o = attention(qkv, batch, seq, c)

        wr = jnp.pad(w_router[i], ((0, 0), (0, LANES - N_EXPERTS)))
        wr_hi = wr.astype(bf16)
        wr_lo = (wr - wr_hi.astype(f32)).astype(bf16)
        br = jnp.pad(b_router[i], (0, LANES - N_EXPERTS)).reshape(1, LANES)
        h1, xn, route, counts = oproj_router(o, h, w_o[i].astype(bf16), g_moe[i], wr_hi, wr_lo, br)

        pos_flat, block_expert, n_used, zero_lo, zero_hi, n_rows = _routing_tables(route, counts, n)
        x_rows = dispatch(xn, pos_flat, zero_lo, zero_hi, n_rows)
        y_rows = experts(x_rows, block_expert, n_used, i, w_gu, b_gu4, w_down, b_down4)
        h = combine_ple(pos_flat, route, h1, y_rows, p.reshape(depth, n, -1), i, g_ple[i],
                        w_ple_gate[i].astype(bf16), w_ple_proj[i].astype(bf16), g_final,
                        final=(i == depth - 1))
    return h.reshape(batch, seq, d)
```
